```python
import math
import jax, jax.numpy as jnp
from jax import lax
import numpy as np

D_MODEL = 2048
BATCH = 16
SEQ = 2048
DEPTH = 4

GRID_W = 64
CTX_LEN = 256
N_MIXERS = 2
N_LAYERS_A = (DEPTH + N_MIXERS - 1) // N_MIXERS
N_LAYERS_B = DEPTH // N_MIXERS
DA_HEADS = 8
DA_DIM = D_MODEL // (2 * DA_HEADS)
GW_HEADS = 32
GW_KV_HEADS = 4
GW_GROUP = GW_HEADS // GW_KV_HEADS
GW_DIM = D_MODEL // GW_HEADS
GW_Q_WIDTH = GW_HEADS * GW_DIM
GW_KV_WIDTH = GW_KV_HEADS * GW_DIM
GW_QKV_WIDTH = GW_Q_WIDTH + 2 * GW_KV_WIDTH
WINDOW = 128
Q_BLOCK = 128
N_GROUPS = 4
EXPERTS_PER_GROUP = 8
N_EXPERTS = N_GROUPS * EXPERTS_PER_GROUP
TOP_K = 2
D_EXPERT = D_MODEL // 4
MOE_BLOCK = 128
ROPE_BASE = 10000.0
RMS_EPS = 1e-6
NEG_INF = -1e30

kernel_name = 'hybrid_diffattn_swa_sink_hmoe_dit'


def _rms_norm(x, g):
    xf = x.astype(jnp.float32)
    y = xf * lax.rsqrt(jnp.mean(xf * xf, axis=-1, keepdims=True) + RMS_EPS)
    return (y * g.astype(jnp.float32)).astype(x.dtype)


def _ada(cond, w, b):
    return jnp.split(jax.nn.silu(cond) @ w + b, 6, axis=-1)


def _modulate(h, shift, scale):
    return h * (1 + scale) + shift


def _axial_rope(rows, dim):
    row = jnp.repeat(jnp.arange(rows, dtype=jnp.float32), GRID_W)
    col = jnp.tile(jnp.arange(GRID_W, dtype=jnp.float32), rows)
    half = dim // 2
    inv = ROPE_BASE ** (-jnp.arange(0, half, 2, dtype=jnp.float32) / half)
    ang = jnp.stack([row[:, None] * inv[None], col[:, None] * inv[None]], axis=1)
    return jnp.cos(ang), jnp.sin(ang)


def _apply_rope(x, cos, sin):
    shp = x.shape
    xr = x.astype(jnp.float32).reshape(shp[:-1] + (2, 2, shp[-1] // 4))
    x1, x2 = xr[..., 0, :], xr[..., 1, :]
    c = cos[None, :, None]
    s = sin[None, :, None]
    out = jnp.stack([x1 * c - x2 * s, x2 * c + x1 * s], axis=-2)
    return out.reshape(shp).astype(x.dtype)


def _diff_attention(hc, hx, w_in, w_out, lam_q1, lam_k1, lam_q2, lam_k2, subln_g,
                    layer_idx, rope, need_ctx):
    B, L, _ = hx.shape
    nb = L // Q_BLOCK
    cos, sin = rope

    def project(h):
        n = h.shape[1]
        q, k, v = jnp.split(h @ w_in, 3, axis=-1)
        return (q.reshape(B, n, 2 * DA_HEADS, DA_DIM),
                k.reshape(B, n, 2 * DA_HEADS, DA_DIM),
                v.reshape(B, n, DA_HEADS, 2 * DA_DIM))

    qc, kc, vc = project(hc)
    qx, kx, vx = project(hx)
    qx = _apply_rope(qx, cos, sin)
    kx = _apply_rope(kx, cos, sin)
    lam_init = 0.8 - 0.6 * math.exp(-0.3 * layer_idx)
    lam = (jnp.exp(jnp.sum(lam_q1.astype(jnp.float32) * lam_k1.astype(jnp.float32)))
           - jnp.exp(jnp.sum(lam_q2.astype(jnp.float32) * lam_k2.astype(jnp.float32)))
           + lam_init)
    scale = DA_DIM ** -0.5

    def attend(q, k, v):
        nq, nk = q.shape[1], k.shape[1]
        s = jnp.einsum('bqhd,bkhd->bhqk', q, k).astype(jnp.float32) * scale
        p = jax.nn.softmax(s, axis=-1).reshape(B, DA_HEADS, 2, nq, nk)
        a = p[:, :, 0] - lam * p[:, :, 1]
        o = jnp.einsum('bhqk,bkhe->bqhe', a.astype(v.dtype), v)
        o = _rms_norm(o, subln_g) * (1.0 - lam_init)
        return o.reshape(B, nq, D_MODEL)

    k_all = jnp.concatenate([kc, kx], axis=1)
    v_all = jnp.concatenate([vc, vx], axis=1)
    q_blocks = jnp.moveaxis(qx.reshape(B, nb, Q_BLOCK, 2 * DA_HEADS, DA_DIM), 1, 0)
    ox = lax.map(lambda qb: attend(qb, k_all, v_all), q_blocks)
    ox = jnp.moveaxis(ox, 0, 1).reshape(B, L, D_MODEL) @ w_out
    oc = attend(qc, kc, vc) @ w_out if need_ctx else None
    return oc, ox


def _window_gqa(hc, hx, w_in, w_out, sinks, rope, need_ctx):
    B, L, _ = hx.shape
    C = hc.shape[1]
    nb = L // Q_BLOCK
    cos, sin = rope

    def project(h):
        n = h.shape[1]
        qkv = h @ w_in
        q = qkv[..., :GW_Q_WIDTH].reshape(B, n, GW_HEADS, GW_DIM)
        k = qkv[..., GW_Q_WIDTH:GW_Q_WIDTH + GW_KV_WIDTH].reshape(B, n, GW_KV_HEADS, GW_DIM)
        v = qkv[..., GW_Q_WIDTH + GW_KV_WIDTH:].reshape(B, n, GW_KV_HEADS, GW_DIM)
        return q, k, v

    qc, kc, vc = project(hc)
    qx, kx, vx = project(hx)
    qx = _apply_rope(qx, cos, sin).reshape(B, L, GW_KV_HEADS, GW_GROUP, GW_DIM)
    kx = _apply_rope(kx, cos, sin)
    qc = qc.reshape(B, C, GW_KV_HEADS, GW_GROUP, GW_DIM)
    sink = sinks.astype(jnp.float32).reshape(GW_KV_HEADS, GW_GROUP)
    scale = GW_DIM ** -0.5

    def attend(q, k, v, mask):
        s = jnp.einsum('bqhgd,bkhd->bhgqk', q, k).astype(jnp.float32) * scale
        if mask is not None:
            s = jnp.where(mask, s, NEG_INF)
        sk = jnp.broadcast_to(sink[None, :, :, None, None], s.shape[:-1] + (1,))
        p = jax.nn.softmax(jnp.concatenate([sk, s], axis=-1), axis=-1)[..., 1:]
        o = jnp.einsum('bhgqk,bkhd->bqhgd', p.astype(v.dtype), v)
        return o.reshape(B, q.shape[1], D_MODEL)

    span = Q_BLOCK + 2 * WINDOW
    pad = ((0, 0), (WINDOW, WINDOW), (0, 0), (0, 0))
    kp = jnp.pad(kx, pad)
    vp = jnp.pad(vx, pad)

    def latent_block(b):
        start = b * Q_BLOCK
        qb = lax.dynamic_slice_in_dim(qx, start, Q_BLOCK, axis=1)
        kw = lax.dynamic_slice_in_dim(kp, start, span, axis=1)
        vw = lax.dynamic_slice_in_dim(vp, start, span, axis=1)
        qi = start + jnp.arange(Q_BLOCK)
        kj = start - WINDOW + jnp.arange(span)
        band = ((kj[None, :] >= 0) & (kj[None, :] < L)
                & (jnp.abs(qi[:, None] - kj[None, :]) <= WINDOW))
        mask = jnp.concatenate([jnp.ones((Q_BLOCK, C), dtype=bool), band], axis=1)
        return attend(qb, jnp.concatenate([kc, kw], axis=1),
                      jnp.concatenate([vc, vw], axis=1), mask)

    ox = lax.map(latent_block, jnp.arange(nb))
    ox = jnp.moveaxis(ox, 0, 1).reshape(B, L, D_MODEL) @ w_out
    oc = attend(qc, kc, vc, None) @ w_out if need_ctx else None
    return oc, ox


def _hier_moe(h, w_group, w_expert, w_gate, w_up, w_down):
    T = h.shape[0]
    hf = h.astype(jnp.float32)
    tok = jnp.arange(T)
    g_logits = hf @ w_group.astype(jnp.float32)
    g_prob = jax.nn.softmax(g_logits, axis=-1)
    g_sel = jnp.argmax(g_logits, axis=-1)
    g_w = g_prob[tok, g_sel]
    e_logits = (hf @ w_expert.astype(jnp.float32)).reshape(T, N_GROUPS, EXPERTS_PER_GROUP)
    e_prob = jax.nn.softmax(e_logits[tok, g_sel], axis=-1)
    top_p, top_i = lax.top_k(e_prob, TOP_K)
    top_p = top_p / jnp.sum(top_p, axis=-1, keepdims=True)
    weights = (g_w[:, None] * top_p).reshape(-1)
    eid = (g_sel[:, None] * EXPERTS_PER_GROUP + top_i).reshape(-1)
    src = jnp.repeat(tok, TOP_K)
    A = T * TOP_K
    order = jnp.argsort(eid)
    eid_s, src_s, w_s = eid[order], src[order], weights[order]
    counts = jnp.bincount(eid, length=N_EXPERTS)
    starts = jnp.cumsum(counts) - counts
    padded = (counts + MOE_BLOCK - 1) // MOE_BLOCK * MOE_BLOCK
    pends = jnp.cumsum(padded)
    pstarts = pends - padded
    dest = pstarts[eid_s] + (jnp.arange(A) - starts[eid_s])
    n_blocks = -(-A // MOE_BLOCK) + N_EXPERTS
    P = n_blocks * MOE_BLOCK
    slot_tok = jnp.zeros((P,), jnp.int32).at[dest].set(src_s.astype(jnp.int32))
    slot_w = jnp.zeros((P,), jnp.float32).at[dest].set(w_s)
    block_expert = jnp.clip(jnp.searchsorted(pends, jnp.arange(n_blocks) * MOE_BLOCK, side='right'),
                            0, N_EXPERTS - 1)

    def expert_block(args):
        idx, w, e = args
        xb = h[idx]
        y = (jax.nn.silu(xb @ w_gate[e]) * (xb @ w_up[e])) @ w_down[e]
        return y.astype(jnp.float32) * w[:, None]

    y = lax.map(expert_block, (slot_tok.reshape(n_blocks, MOE_BLOCK),
                               slot_w.reshape(n_blocks, MOE_BLOCK), block_expert))
    out = jnp.zeros((T, D_MODEL), jnp.float32).at[slot_tok].add(y.reshape(P, D_MODEL))
    return out.astype(h.dtype)


def setup_inputs(seed: int = 0) -> dict:
    key = jax.random.key(seed)
    ks = jax.random.split(key, 26)
    D = D_MODEL

    def nrm(k, shape, s):
        return jax.random.normal(k, shape, jnp.float32) * s

    return {
        'x': nrm(ks[0], (BATCH, SEQ, D), 1.0),
        'c': nrm(ks[1], (BATCH, D), 1.0),
        'ctx': nrm(ks[2], (BATCH, CTX_LEN, D), 1.0),
        'c_ctx': nrm(ks[3], (D,), 1.0),
        'ada_w': nrm(ks[4], (DEPTH, D, 6 * D), 0.5 * D ** -0.5),
        'ada_b': nrm(ks[5], (DEPTH, 6 * D), 0.01),
        'norm_mix_g': 1.0 + nrm(ks[6], (DEPTH, D), 0.02),
        'norm_ffn_g': 1.0 + nrm(ks[7], (DEPTH, D), 0.02),
        'norm_out_g': 1.0 + nrm(ks[8], (D,), 0.02),
        'da_w_in': nrm(ks[9], (N_LAYERS_A, D, 3 * D), D ** -0.5),
        'da_w_out': nrm(ks[10], (N_LAYERS_A, D, D), D ** -0.5),
        'da_lam_q1': nrm(ks[11], (N_LAYERS_A, DA_DIM), 0.1),
        'da_lam_k1': nrm(ks[12], (N_LAYERS_A, DA_DIM), 0.1),
        'da_lam_q2': nrm(ks[13], (N_LAYERS_A, DA_DIM), 0.1),
        'da_lam_k2': nrm(ks[14], (N_LAYERS_A, DA_DIM), 0.1),
        'da_subln_g': 1.0 + nrm(ks[15], (N_LAYERS_A, 2 * DA_DIM), 0.02),
        'gw_w_in': nrm(ks[16], (N_LAYERS_B, D, GW_QKV_WIDTH), D ** -0.5),
        'gw_w_out': nrm(ks[17], (N_LAYERS_B, D, D), D ** -0.5),
        'gw_sinks': nrm(ks[18], (N_LAYERS_B, GW_HEADS), 0.5),
        'moe_w_group': nrm(ks[19], (DEPTH, D, N_GROUPS), D ** -0.5),
        'moe_w_expert': nrm(ks[20], (DEPTH, D, N_EXPERTS), D ** -0.5),
        'moe_w_gate': nrm(ks[21], (DEPTH, N_EXPERTS, D, D_EXPERT), D ** -0.5),
        'moe_w_up': nrm(ks[22], (DEPTH, N_EXPERTS, D, D_EXPERT), D ** -0.5),
        'moe_w_down': nrm(ks[23], (DEPTH, N_EXPERTS, D_EXPERT, D), D_EXPERT ** -0.5),
    }


def reference(x, c, ctx, c_ctx, ada_w, ada_b, norm_mix_g, norm_ffn_g, norm_out_g,
              da_w_in, da_w_out, da_lam_q1, da_lam_k1, da_lam_q2, da_lam_k2, da_subln_g,
              gw_w_in, gw_w_out, gw_sinks,
              moe_w_group, moe_w_expert, moe_w_gate, moe_w_up, moe_w_down):
    B, L, D = x.shape
    C = ctx.shape[1]
    ROWS = L // GRID_W
    rope_a = _axial_rope(ROWS, DA_DIM)
    rope_b = _axial_rope(ROWS, GW_DIM)
    xc = ctx
    for i in range(DEPTH):
        need_ctx = i < DEPTH - 1
        sh1, sc1, g1, sh2, sc2, g2 = _ada(c, ada_w[i], ada_b[i])
        csh1, csc1, cg1, csh2, csc2, cg2 = _ada(c_ctx, ada_w[i], ada_b[i])
        hx = _modulate(_rms_norm(x, norm_mix_g[i]), sh1[:, None], sc1[:, None])
        hc = _modulate(_rms_norm(xc, norm_mix_g[i]), csh1, csc1)
        j = i // N_MIXERS
        if i % N_MIXERS == 0:
            oc, ox = _diff_attention(hc, hx, da_w_in[j], da_w_out[j], da_lam_q1[j], da_lam_k1[j],
                                     da_lam_q2[j], da_lam_k2[j], da_subln_g[j], i, rope_a, need_ctx)
        else:
            oc, ox = _window_gqa(hc, hx, gw_w_in[j], gw_w_out[j], gw_sinks[j], rope_b, need_ctx)
        x = x + g1[:, None] * ox
        hx = _modulate(_rms_norm(x, norm_ffn_g[i]), sh2[:, None], sc2[:, None])
        if need_ctx:
            xc = xc + cg1 * oc
            hc = _modulate(_rms_norm(xc, norm_ffn_g[i]), csh2, csc2)
            tokens = jnp.concatenate([hc.reshape(B * C, D), hx.reshape(B * L, D)], axis=0)
            y = _hier_moe(tokens, moe_w_group[i], moe_w_expert[i], moe_w_gate[i], moe_w_up[i], moe_w_down[i])
            xc = xc + cg2 * y[:B * C].reshape(B, C, D)
            yx = y[B * C:]
        else:
            yx = _hier_moe(hx.reshape(B * L, D), moe_w_group[i], moe_w_expert[i],
                           moe_w_gate[i], moe_w_up[i], moe_w_down[i])
        x = x + g2[:, None] * yx.reshape(B, L, D)
    return _rms_norm(x, norm_out_g)
```

```python
import functools
import math

import jax
import jax.numpy as jnp
from jax import lax
from jax.experimental import pallas as pl
from jax.experimental.pallas import tpu as pltpu

F32 = jnp.float32
BF16 = jnp.bfloat16

DEPTH = 4
N_MIXERS = 2
GRID_W = 64
DA_HEADS = 8
GW_HEADS = 32
GW_KV_HEADS = 4
GW_GROUP = GW_HEADS // GW_KV_HEADS
WINDOW = 128
N_GROUPS = 4
EXPERTS_PER_GROUP = 8
N_EXPERTS = N_GROUPS * EXPERTS_PER_GROUP
TOP_K = 2
ROPE_BASE = 10000.0
RMS_EPS = 1e-6
NEG_INF = -1e30

LANES = 128
COND_ROWS = 32
N_MOD = 6
ROUTER_LANES = 128
VMEM_LIMIT = 56 * 1024 * 1024


def _cparams(n_axes, vmem=VMEM_LIMIT):
    return pltpu.CompilerParams(dimension_semantics=("arbitrary",) * n_axes, vmem_limit_bytes=vmem)


def _dot(a, b):
    return jnp.dot(a, b, preferred_element_type=F32)


def _dot_nt(a, b):
    return lax.dot_general(a, b, (((1,), (1,)), ((), ())), preferred_element_type=F32)


def _ada_kernel(c_ref, w_ref, b_ref, o_ref):
    c = c_ref[...]
    s = (c / (1.0 + jnp.exp(-c))).astype(BF16)
    o_ref[...] = _dot(s, w_ref[...].astype(BF16)) + b_ref[...]


def _ada_call(cond, ada_w, ada_b):
    depth, d, n = ada_w.shape
    tn = 1024
    return pl.pallas_call(
        _ada_kernel,
        grid=(depth, n // tn),
        in_specs=[
            pl.BlockSpec((COND_ROWS, d), lambda l, j: (0, 0)),
            pl.BlockSpec((None, d, tn), lambda l, j: (l, 0, j)),
            pl.BlockSpec((None, 1, tn), lambda l, j: (l, 0, j)),
        ],
        out_specs=pl.BlockSpec((None, COND_ROWS, tn), lambda l, j: (l, 0, j)),
        out_shape=jax.ShapeDtypeStruct((depth, COND_ROWS, n), F32),
        compiler_params=_cparams(2),
        name="ada_mod",
    )(cond, ada_w, ada_b.reshape(depth, 1, n))


def _rope_tables(seq, dim):
    rows = seq // GRID_W
    row = jnp.repeat(jnp.arange(rows, dtype=F32), GRID_W)
    col = jnp.tile(jnp.arange(GRID_W, dtype=F32), rows)
    half = dim // 2
    inv = ROPE_BASE ** (-jnp.arange(0, half, 2, dtype=F32) / half)
    ang_r = row[:, None] * inv[None]
    ang_c = col[:, None] * inv[None]
    cr, sr, cc, sc = jnp.cos(ang_r), jnp.sin(ang_r), jnp.cos(ang_c), jnp.sin(ang_c)
    z = jnp.zeros_like(sr)
    reps = LANES // dim
    cos_t = jnp.tile(jnp.concatenate([cr, cr, cc, cc], axis=1), (1, reps))
    sin_a = jnp.tile(jnp.concatenate([-sr, z, -sc, z], axis=1), (1, reps))
    sin_b = jnp.tile(jnp.concatenate([z, sr, z, sc], axis=1), (1, reps))
    return cos_t, sin_a, sin_b


def _qkv_kernel(x_ref, sh_ref, sc_ref, g_ref, w_ref, cos_ref, sa_ref, sb_ref, o_ref, h_scr, *,
                ncb, n_rope_full, partial_cols, nq, qscale, rope_q, tn):
    r = pl.program_id(0)
    j = pl.program_id(1)

    @pl.when(j == 0)
    def _():
        x = x_ref[...]
        ms = jnp.mean(x * x, axis=-1, keepdims=True)
        y = x * lax.rsqrt(ms + RMS_EPS) * g_ref[...]
        h_scr[...] = (y * (1.0 + sc_ref[0]) + sh_ref[0]).astype(BF16)

    acc = _dot(h_scr[...], w_ref[...])
    if qscale != 1.0:
        acc = acc * jnp.where(j < nq, jnp.float32(qscale), jnp.float32(1.0))
    is_lat = r >= ncb

    def store(ncols):
        for ci in range(tn // LANES):
            a = acc[:, ci * LANES:(ci + 1) * LANES]
            if ci * LANES < ncols:
                a = (a * cos_ref[...] + pltpu.roll(a, LANES - rope_q, 1) * sa_ref[...]
                     + pltpu.roll(a, rope_q, 1) * sb_ref[...])
            o_ref[:, ci * LANES:(ci + 1) * LANES] = a.astype(BF16)

    full = jnp.logical_and(is_lat, j < n_rope_full)
    if partial_cols:
        part = jnp.logical_and(is_lat, j == n_rope_full)
        plain = jnp.logical_not(jnp.logical_or(full, part))
        pl.when(part)(lambda: store(partial_cols))
    else:
        plain = jnp.logical_not(full)
    pl.when(full)(lambda: store(tn))
    pl.when(plain)(lambda: store(0))


def _qkv_call(xa, modr, gain, w, tables, *, layer, n_ctx_rows, seq, n_rope_cols, n_q_cols, qscale, rope_dim):
    t, d = xa.shape
    n = w.shape[1]
    tn = 512
    tm = min(1024, seq, n_ctx_rows)
    assert t % tm == 0 and n_ctx_rows % tm == 0 and seq % tm == 0 and n % tn == 0 and n_q_cols % tn == 0
    ncb = n_ctx_rows // tm
    bpb = seq // tm
    base = layer * COND_ROWS

    def seg(r):
        return jnp.where(r < ncb, COND_ROWS // 2, (r - ncb) // bpb)

    def pos(r):
        return jnp.where(r < ncb, 0, (r - ncb) % bpb)

    kern = functools.partial(
        _qkv_kernel, ncb=ncb, n_rope_full=n_rope_cols // tn, partial_cols=n_rope_cols % tn,
        nq=n_q_cols // tn, qscale=qscale, rope_q=rope_dim // 4, tn=tn)
    tab_spec = pl.BlockSpec((tm, LANES), lambda r, j: (pos(r), 0))
    return pl.pallas_call(
        kern,
        grid=(t // tm, n // tn),
        in_specs=[
            pl.BlockSpec((tm, d), lambda r, j: (r, 0)),
            pl.BlockSpec((1, 1, d), lambda r, j: ((base + seg(r)) * N_MOD + 0, 0, 0)),
            pl.BlockSpec((1, 1, d), lambda r, j: ((base + seg(r)) * N_MOD + 1, 0, 0)),
            pl.BlockSpec((1, d), lambda r, j: (0, 0)),
            pl.BlockSpec((d, tn), lambda r, j: (0, j)),
            tab_spec, tab_spec, tab_spec,
        ],
        out_specs=pl.BlockSpec((tm, tn), lambda r, j: (r, j)),
        out_shape=jax.ShapeDtypeStruct((t, n), BF16),
        scratch_shapes=[pltpu.VMEM((tm, d), BF16)],
        compiler_params=_cparams(2),
        name=f"qkv_proj_l{layer}",
    )(xa, modr, modr, gain.reshape(1, d), w, *tables)


def _da_body(q_ref, kc_ref, vc_ref, kx_ref, vx_ref, lq1, lk1, lq2, lk2, g_ref, o_ref, *, lam_init, has_x):
    dh = q_ref.shape[1] // 2
    lam = (jnp.exp(jnp.sum(lq1[...] * lk1[...], axis=-1, keepdims=True))
           - jnp.exp(jnp.sum(lq2[...] * lk2[...], axis=-1, keepdims=True)) + lam_init)

    def probs(lo):
        qc = q_ref[:, lo:lo + dh]
        s_c = _dot_nt(qc, kc_ref[:, lo:lo + dh])
        m = jnp.max(s_c, axis=-1, keepdims=True)
        if has_x:
            s_x = _dot_nt(qc, kx_ref[:, lo:lo + dh])
            m = jnp.maximum(m, jnp.max(s_x, axis=-1, keepdims=True))
        e_c = jnp.exp(s_c - m)
        den = jnp.sum(e_c, axis=-1, keepdims=True)
        e_x = None
        if has_x:
            e_x = jnp.exp(s_x - m)
            den = den + jnp.sum(e_x, axis=-1, keepdims=True)
        return e_c, e_x, 1.0 / den

    e1c, e1x, r1 = probs(0)
    e2c, e2x, r2 = probs(dh)
    w2 = lam * r2
    o = _dot((e1c * r1 - e2c * w2).astype(BF16), vc_ref[...])
    if has_x:
        o = o + _dot((e1x * r1 - e2x * w2).astype(BF16), vx_ref[...])
    ms = jnp.mean(o * o, axis=-1, keepdims=True)
    o = o * lax.rsqrt(ms + RMS_EPS) * g_ref[...] * (1.0 - lam_init)
    o_ref[...] = o.astype(BF16)


def _da_kernel(*refs, lam_init, nqb, has_ctx_steps):
    body = functools.partial(_da_body, *refs, lam_init=lam_init)
    if has_ctx_steps:
        qb = pl.program_id(2)
        pl.when(qb < nqb)(lambda: body(has_x=True))
        pl.when(qb >= nqb)(lambda: body(has_x=False))
    else:
        body(has_x=True)


def _da_attention(qkv, lam_vecs, subln_g, *, layer, batch, seq, n_ctx, need_ctx):
    t, n3 = qkv.shape
    d = n3 // 3
    hw = d // DA_HEADS
    tc_rows = batch * n_ctx
    tq = min(256, seq)
    assert seq % tq == 0 and n_ctx % tq == 0 and tc_rows % seq == 0 and hw % LANES == 0
    lam_init = 0.8 - 0.6 * math.exp(-0.3 * layer)
    nh = DA_HEADS
    small = [v.reshape(1, -1) for v in lam_vecs] + [subln_g.reshape(1, hw)]
    small_specs = [pl.BlockSpec(v.shape, lambda b, h, qb: (0, 0)) for v in small]
    qoff = tc_rows // tq
    nqb = seq // tq
    ncq = n_ctx // tq if need_ctx else 0
    xoff = tc_rows // seq

    def q_row(b, qb):
        lat = qoff + b * nqb + jnp.minimum(qb, nqb - 1)
        return jnp.where(qb < nqb, lat, b * ncq + (qb - nqb)) if need_ctx else lat

    def o_row(b, qb):
        return q_row(b, qb) if need_ctx else b * nqb + qb

    return pl.pallas_call(
        functools.partial(_da_kernel, lam_init=lam_init, nqb=nqb, has_ctx_steps=need_ctx),
        grid=(batch, nh, nqb + ncq),
        in_specs=[
            pl.BlockSpec((tq, hw), lambda b, h, qb: (q_row(b, qb), h)),
            pl.BlockSpec((n_ctx, hw), lambda b, h, qb: (b, nh + h)),
            pl.BlockSpec((n_ctx, hw), lambda b, h, qb: (b, 2 * nh + h)),
            pl.BlockSpec((seq, hw), lambda b, h, qb: (xoff + b, nh + h)),
            pl.BlockSpec((seq, hw), lambda b, h, qb: (xoff + b, 2 * nh + h)),
        ] + small_specs,
        out_specs=pl.BlockSpec((tq, hw), lambda b, h, qb: (o_row(b, qb), h)),
        out_shape=jax.ShapeDtypeStruct((t if need_ctx else batch * seq, d), BF16),
        compiler_params=_cparams(3),
        name=f"diff_attn_l{layer}",
    )(qkv, qkv, qkv, qkv, qkv, *small)


def _gw_body(sink_ref, q_ref, kc_ref, vc_ref, kx_ref, vx_ref, o_ref, *, has_x, heads_per_step, group, dim, nqb):
    gp = pl.program_id(1)
    tq = q_ref.shape[0]
    kc = kc_ref[...]
    vc = vc_ref[...]
    if has_x:
        qb = pl.program_id(2)
        rows = lax.broadcasted_iota(jnp.int32, (tq, tq), 0)
        cols = lax.broadcasted_iota(jnp.int32, (tq, tq), 1)
        mask_p = jnp.logical_and(cols >= rows, qb >= 1)
        mask_n = jnp.logical_and(cols <= rows, qb + 1 < nqb)
        st_p = pl.multiple_of(jnp.maximum(qb - 1, 0) * tq, tq)
        st_m = pl.multiple_of(qb * tq, tq)
        st_n = pl.multiple_of(jnp.minimum(qb + 1, nqb - 1) * tq, tq)
        kx = [kx_ref[pl.ds(s, tq), :] for s in (st_p, st_m, st_n)]
        vx = [vx_ref[pl.ds(s, tq), :] for s in (st_p, st_m, st_n)]
        masks = (mask_p, None, mask_n)
    for hh in range(heads_per_step):
        gi = hh // group
        ks = slice(gi * dim, (gi + 1) * dim)
        qh = q_ref[:, hh * dim:(hh + 1) * dim]
        sk = sink_ref[gp * heads_per_step + hh]
        scores = [_dot_nt(qh, kc[:, ks])]
        vals = [vc[:, ks]]
        if has_x:
            for kk, vv, mk in zip(kx, vx, masks):
                s = _dot_nt(qh, kk[:, ks])
                if mk is not None:
                    s = jnp.where(mk, s, NEG_INF)
                scores.append(s)
                vals.append(vv[:, ks])
        m = jnp.max(scores[0], axis=-1, keepdims=True)
        for s in scores[1:]:
            m = jnp.maximum(m, jnp.max(s, axis=-1, keepdims=True))
        m = jnp.maximum(m, sk)
        den = jnp.exp(sk - m)
        o = None
        for s, v in zip(scores, vals):
            e = jnp.exp(s - m)
            den = den + jnp.sum(e, axis=-1, keepdims=True)
            pv = _dot(e.astype(BF16), v)
            o = pv if o is None else o + pv
        o_ref[:, hh * dim:(hh + 1) * dim] = (o * (1.0 / den)).astype(BF16)


def _gw_kernel(*refs, nqb, has_ctx_steps, **kw):
    body = functools.partial(_gw_body, *refs, nqb=nqb, **kw)
    if has_ctx_steps:
        qb = pl.program_id(2)
        pl.when(qb < nqb)(lambda: body(has_x=True))
        pl.when(qb >= nqb)(lambda: body(has_x=False))
    else:
        body(has_x=True)


def _gw_attention(qkv, sinks, *, layer, batch, seq, n_ctx, need_ctx, d):
    t = qkv.shape[0]
    dim = d // GW_HEADS
    kv_per_step = LANES // dim
    hps = kv_per_step * GW_GROUP
    qw = hps * dim
    n_gp = GW_KV_HEADS // kv_per_step
    tq = WINDOW
    tc_rows = batch * n_ctx
    assert seq % tq == 0 and n_ctx % tq == 0 and tc_rows % seq == 0
    nqb = seq // tq
    ncq = n_ctx // tq if need_ctx else 0
    qoff = tc_rows // tq
    xoff = tc_rows // seq
    kcol = d // LANES
    vcol = kcol + GW_KV_HEADS * dim // LANES

    def q_row(b, qb):
        lat = qoff + b * nqb + jnp.minimum(qb, nqb - 1)
        return jnp.where(qb < nqb, lat, b * ncq + (qb - nqb)) if need_ctx else lat

    def o_row(b, qb):
        return q_row(b, qb) if need_ctx else b * nqb + qb

    return pl.pallas_call(
        functools.partial(_gw_kernel, nqb=nqb, has_ctx_steps=need_ctx, heads_per_step=hps, group=GW_GROUP, dim=dim),
        grid=(batch, n_gp, nqb + ncq),
        in_specs=[
            pl.BlockSpec(memory_space=pltpu.SMEM),
            pl.BlockSpec((tq, qw), lambda b, g, qb: (q_row(b, qb), g)),
            pl.BlockSpec((n_ctx, LANES), lambda b, g, qb: (b, kcol + g)),
            pl.BlockSpec((n_ctx, LANES), lambda b, g, qb: (b, vcol + g)),
            pl.BlockSpec((seq, LANES), lambda b, g, qb: (xoff + b, kcol + g)),
            pl.BlockSpec((seq, LANES), lambda b, g, qb: (xoff + b, vcol + g)),
        ],
        out_specs=pl.BlockSpec((tq, qw), lambda b, g, qb: (o_row(b, qb), g)),
        out_shape=jax.ShapeDtypeStruct((t if need_ctx else batch * seq, d), BF16),
        compiler_params=_cparams(3),
        name=f"gqa_attn_l{layer}",
    )(sinks, qkv, qkv, qkv, qkv, qkv)


def _oproj_kernel(o_ref, w_ref, x_ref, g1_ref, sh_ref, sc_ref, gn_ref, rhi_ref, rlo_ref,
                  xo_ref, h_ref, lg_ref):
    xn = x_ref[...] + g1_ref[0] * _dot(o_ref[...], w_ref[...])
    xo_ref[...] = xn
    ms = jnp.mean(xn * xn, axis=-1, keepdims=True)
    h = xn * lax.rsqrt(ms + RMS_EPS) * gn_ref[...]
    h = h * (1.0 + sc_ref[0]) + sh_ref[0]
    h_ref[...] = h
    h_hi = h.astype(BF16)
    h_lo = (h - h_hi.astype(F32)).astype(BF16)
    lg_ref[...] = _dot(h_hi, rhi_ref[...]) + (_dot(h_hi, rlo_ref[...]) + _dot(h_lo, rhi_ref[...]))


def _oproj_call(o, w, xa, modr, gain, r_hi, r_lo, *, layer, n_ctx_rows, seq, latent_only):
    t, d = xa.shape
    tm = 256
    assert n_ctx_rows % tm == 0 and seq % tm == 0
    ncb = n_ctx_rows // tm
    bpb = seq // tm
    off = ncb if latent_only else 0
    nblk = t // tm - off
    base = layer * COND_ROWS

    def mrow(r, chunk):
        blk = r + off
        seg = jnp.where(blk < ncb, COND_ROWS // 2, (blk - ncb) // bpb)
        return ((base + seg) * N_MOD + chunk, 0, 0)

    const = lambda r: (0, 0)
    return pl.pallas_call(
        _oproj_kernel,
        grid=(nblk,),
        in_specs=[
            pl.BlockSpec((tm, d), lambda r: (r, 0)),
            pl.BlockSpec((d, d), const, pipeline_mode=pl.Buffered(1)),
            pl.BlockSpec((tm, d), lambda r: (r + off, 0)),
            pl.BlockSpec((1, 1, d), lambda r: mrow(r, 2)),
            pl.BlockSpec((1, 1, d), lambda r: mrow(r, 3)),
            pl.BlockSpec((1, 1, d), lambda r: mrow(r, 4)),
            pl.BlockSpec((1, d), const),
            pl.BlockSpec((d, ROUTER_LANES), const),
            pl.BlockSpec((d, ROUTER_LANES), const),
        ],
        out_specs=[
            pl.BlockSpec((tm, d), lambda r: (r + off, 0)),
            pl.BlockSpec((tm, d), lambda r: (r, 0)),
            pl.BlockSpec((tm, ROUTER_LANES), lambda r: (r, 0)),
        ],
        out_shape=[
            jax.ShapeDtypeStruct((t, d), F32),
            jax.ShapeDtypeStruct((nblk * tm, d), F32),
            jax.ShapeDtypeStruct((nblk * tm, ROUTER_LANES), F32),
        ],
        input_output_aliases={2: 0},
        compiler_params=_cparams(1),
        name=f"out_proj_l{layer}",
    )(o, w, xa, modr, modr, modr, gain.reshape(1, d), r_hi, r_lo)


def _route(logits, tb):
    t = logits.shape[0]
    gl = logits[:, :N_GROUPS]
    g_sel = jnp.argmax(gl, axis=-1)
    g_prob = jax.nn.softmax(gl, axis=-1)
    g_w = jnp.take_along_axis(g_prob, g_sel[:, None], axis=1)[:, 0]
    el = logits[:, N_GROUPS:N_GROUPS + N_EXPERTS].reshape(t, N_GROUPS, EXPERTS_PER_GROUP)
    e_sel = jnp.take_along_axis(el, g_sel[:, None, None], axis=1)[:, 0]
    e_prob = jax.nn.softmax(e_sel, axis=-1)
    top_p, top_i = lax.top_k(e_prob, TOP_K)
    top_p = top_p / jnp.sum(top_p, axis=-1, keepdims=True)
    w = (g_w[:, None] * top_p).reshape(-1)
    eid = (g_sel[:, None] * EXPERTS_PER_GROUP + top_i).reshape(-1).astype(jnp.int32)
    a = t * TOP_K
    onehot = (eid[:, None] == jnp.arange(N_EXPERTS, dtype=jnp.int32)[None]).astype(jnp.int32)
    csum = jnp.cumsum(onehot, axis=0)
    rank = jnp.take_along_axis(csum, eid[:, None], axis=1)[:, 0] - 1
    counts = csum[-1]
    nblk = (counts + tb - 1) // tb
    bend = jnp.cumsum(nblk)
    bstart = bend - nblk
    dest = bstart[eid] * tb + rank
    n_blocks = -(-a // tb) + N_EXPERTS
    p = n_blocks * tb
    tok = jnp.arange(a, dtype=jnp.int32) // TOP_K
    slot_tok = jnp.zeros((p,), jnp.int32).at[dest].set(tok, unique_indices=True)
    slot_w = jnp.zeros((p,), F32).at[dest].set(w, unique_indices=True)
    block_expert = jnp.clip(jnp.searchsorted(bend, jnp.arange(n_blocks, dtype=jnp.int32), side="right"),
                            0, N_EXPERTS - 1).astype(jnp.int32)
    nact = bend[-1:].astype(jnp.int32)
    return slot_tok, slot_w, block_expert, nact, dest.astype(jnp.int32), n_blocks


def _row_gather_start(idx_ref, n_rows, src_hbm, dst, sem, *, idx_stride=1, idx_off=0):
    def body(r, carry):
        tok = idx_ref[0, 0, r * idx_stride + idx_off]
        pltpu.make_async_copy(src_hbm.at[pl.ds(tok, 1)], dst.at[pl.ds(r, 1)], sem).start()
        return carry
    lax.fori_loop(0, n_rows, body, 0, unroll=8)


def _row_gather_wait(n_rows, src_hbm, dst, sem):
    pltpu.make_async_copy(src_hbm.at[pl.ds(0, n_rows)], dst, sem).wait()


def _moe_kernel(be_ref, nact_ref, st_ref, h_hbm, sw_ref, wg_ref, wu_ref, wd_ref, y_ref, xbuf, sem):
    del be_ref
    s = pl.program_id(0)
    nact = nact_ref[0]
    tb = xbuf.shape[1]

    @pl.when(s < nact)
    def _():
        slot = s % 2
        _row_gather_start(st_ref, tb, h_hbm, xbuf.at[slot], sem.at[slot])

    blk = s - 1

    @pl.when(jnp.logical_and(blk >= 0, blk < nact))
    def _():
        slot = blk % 2
        _row_gather_wait(tb, h_hbm, xbuf.at[slot], sem.at[slot])
        xb = xbuf[slot].astype(BF16)
        g = _dot(xb, wg_ref[...])
        u = _dot(xb, wu_ref[...])
        act = (g / (1.0 + jnp.exp(-g))) * u
        y = _dot(act.astype(BF16), wd_ref[...])
        y_ref[...] = y * sw_ref[...]

    @pl.when(jnp.logical_and(blk >= 0, blk >= nact))
    def _():
        y_ref[...] = jnp.zeros_like(y_ref)


def _moe_call(h, slot_tok, slot_w, block_expert, nact, wg, wu, wd, *, n_blocks, tb, layer):
    t, d = h.shape
    de = wg.shape[2]
    p = n_blocks * tb
    prev = lambda s: jnp.maximum(s - 1, 0)
    grid_spec = pltpu.PrefetchScalarGridSpec(
        num_scalar_prefetch=2,
        grid=(n_blocks + 1,),
        in_specs=[
            pl.BlockSpec((1, 1, tb), lambda s, be, na: (jnp.minimum(s, n_blocks - 1), 0, 0),
                         memory_space=pltpu.SMEM),
            pl.BlockSpec(memory_space=pl.ANY),
            pl.BlockSpec((tb, 1), lambda s, be, na: (prev(s), 0)),
            pl.BlockSpec((None, d, de), lambda s, be, na: (be[prev(s)], 0, 0)),
            pl.BlockSpec((None, d, de), lambda s, be, na: (be[prev(s)], 0, 0)),
            pl.BlockSpec((None, de, d), lambda s, be, na: (be[prev(s)], 0, 0)),
        ],
        out_specs=pl.BlockSpec((tb, d), lambda s, be, na: (prev(s), 0)),
        scratch_shapes=[pltpu.VMEM((2, tb, d), F32), pltpu.SemaphoreType.DMA((2,))],
    )
    return pl.pallas_call(
        _moe_kernel,
        grid_spec=grid_spec,
        out_shape=jax.ShapeDtypeStruct((p, d), F32),
        compiler_params=_cparams(1),
        name=f"moe_experts_l{layer}",
    )(block_expert, nact, slot_tok.reshape(n_blocks, 1, tb), h, slot_w.reshape(p, 1), wg, wu, wd)


def _combine_kernel(pos_ref, y_hbm, x_ref, g2_ref, gout_ref, o_ref, ybuf, sem, *, final_norm):
    s = pl.program_id(0)
    nb = pl.num_programs(0) - 1
    tc = ybuf.shape[2]

    @pl.when(s < nb)
    def _():
        slot = s % 2
        for k in range(TOP_K):
            _row_gather_start(pos_ref, tc, y_hbm, ybuf.at[slot, k], sem.at[slot], idx_stride=TOP_K, idx_off=k)

    @pl.when(s >= 1)
    def _():
        slot = (s - 1) % 2
        for k in range(TOP_K):
            _row_gather_wait(tc, y_hbm, ybuf.at[slot, k], sem.at[slot])
        xn = x_ref[...] + g2_ref[0] * (ybuf[slot, 0] + ybuf[slot, 1])
        if final_norm:
            ms = jnp.mean(xn * xn, axis=-1, keepdims=True)
            xn = xn * lax.rsqrt(ms + RMS_EPS) * gout_ref[...]
        o_ref[...] = xn


def _combine_call(y, dest, xa, modr, gout, *, layer, n_ctx_rows, seq, latent_only, final_norm):
    t, d = xa.shape
    tc = 256
    assert n_ctx_rows % tc == 0 and seq % tc == 0
    ncb = n_ctx_rows // tc
    bpb = seq // tc
    off = ncb if latent_only else 0
    nblk = t // tc - off
    base = layer * COND_ROWS
    prev = lambda s: jnp.maximum(s - 1, 0)

    def mrow(s):
        blk = prev(s) + off
        seg = jnp.where(blk < ncb, COND_ROWS // 2, (blk - ncb) // bpb)
        return ((base + seg) * N_MOD + 5, 0, 0)

    out_off = 0 if final_norm else off
    out_rows = nblk * tc if final_norm else t
    return pl.pallas_call(
        functools.partial(_combine_kernel, final_norm=final_norm),
        grid=(nblk + 1,),
        in_specs=[
            pl.BlockSpec((1, 1, TOP_K * tc), lambda s: (jnp.minimum(s, nblk - 1), 0, 0), memory_space=pltpu.SMEM),
            pl.BlockSpec(memory_space=pl.ANY),
            pl.BlockSpec((tc, d), lambda s: (prev(s) + off, 0)),
            pl.BlockSpec((1, 1, d), mrow),
            pl.BlockSpec((1, d), lambda s: (0, 0)),
        ],
        out_specs=pl.BlockSpec((tc, d), lambda s: (prev(s) + out_off, 0)),
        out_shape=jax.ShapeDtypeStruct((out_rows, d), F32),
        scratch_shapes=[pltpu.VMEM((2, TOP_K, tc, d), F32), pltpu.SemaphoreType.DMA((2,))],
        input_output_aliases={} if final_norm else {2: 0},
        compiler_params=_cparams(1),
        name=f"moe_combine_l{layer}",
    )(dest.reshape(nblk, 1, TOP_K * tc), y, xa, modr, gout.reshape(1, d))


def kernel(x, c, ctx, c_ctx, ada_w, ada_b, norm_mix_g, norm_ffn_g, norm_out_g, da_w_in, da_w_out,
           da_lam_q1, da_lam_k1, da_lam_q2, da_lam_k2, da_subln_g, gw_w_in, gw_w_out, gw_sinks,
           moe_w_group, moe_w_expert, moe_w_gate, moe_w_up, moe_w_down):
    batch, seq, d = x.shape
    n_ctx = ctx.shape[1]
    tc_rows = batch * n_ctx
    depth = ada_w.shape[0]
    assert batch < COND_ROWS // 2 + 1 and depth == DEPTH

    xa = jnp.concatenate([ctx.reshape(tc_rows, d), x.reshape(batch * seq, d)], axis=0)
    cond = jnp.zeros((COND_ROWS, d), F32).at[:batch].set(c).at[COND_ROWS // 2].set(c_ctx)
    mod = _ada_call(cond, ada_w, ada_b)
    modr = mod.reshape(depth * COND_ROWS * N_MOD, 1, d)

    da_dim = d // (2 * DA_HEADS)
    gw_dim = d // GW_HEADS
    tables_a = _rope_tables(seq, da_dim)
    tables_b = _rope_tables(seq, gw_dim)
    tb = 256
    out = None
    for i in range(depth):
        last = i == depth - 1
        need_ctx = not last
        jm = i // N_MIXERS
        if i % N_MIXERS == 0:
            qkv = _qkv_call(xa, modr, norm_mix_g[i], da_w_in[jm].astype(BF16), tables_a, layer=i,
                            n_ctx_rows=tc_rows, seq=seq, n_rope_cols=2 * d, n_q_cols=d,
                            qscale=da_dim ** -0.5, rope_dim=da_dim)
            o = _da_attention(qkv, (da_lam_q1[jm], da_lam_k1[jm], da_lam_q2[jm], da_lam_k2[jm]),
                              da_subln_g[jm], layer=i, batch=batch, seq=seq, n_ctx=n_ctx, need_ctx=need_ctx)
            w_out = da_w_out[jm]
        else:
            kv_w = GW_KV_HEADS * gw_dim
            qkv = _qkv_call(xa, modr, norm_mix_g[i], gw_w_in[jm].astype(BF16), tables_b, layer=i,
                            n_ctx_rows=tc_rows, seq=seq, n_rope_cols=d + kv_w, n_q_cols=d,
                            qscale=gw_dim ** -0.5, rope_dim=gw_dim)
            o = _gw_attention(qkv, gw_sinks[jm], layer=i, batch=batch, seq=seq, n_ctx=n_ctx,
                              need_ctx=need_ctx, d=d)
            w_out = gw_w_out[jm]
        w_r = jnp.concatenate([moe_w_group[i], moe_w_expert[i]], axis=1)
        w_r = jnp.pad(w_r, ((0, 0), (0, ROUTER_LANES - w_r.shape[1])))
        r_hi = w_r.astype(BF16)
        r_lo = (w_r - r_hi.astype(F32)).astype(BF16)
        xa, h2, logits = _oproj_call(o, w_out.astype(BF16), xa, modr, norm_ffn_g[i], r_hi, r_lo, layer=i,
                                     n_ctx_rows=tc_rows, seq=seq, latent_only=last)
        slot_tok, slot_w, block_expert, nact, dest, n_blocks = _route(logits, tb)
        y = _moe_call(h2, slot_tok, slot_w, block_expert, nact, moe_w_gate[i].astype(BF16),
                      moe_w_up[i].astype(BF16), moe_w_down[i].astype(BF16), n_blocks=n_blocks, tb=tb, layer=i)
        res = _combine_call(y, dest, xa, modr, norm_out_g, layer=i, n_ctx_rows=tc_rows, seq=seq,
                            latent_only=last, final_norm=last)
        if last:
            out = res
        else:
            xa = res
    return out.reshape(batch, seq, d)
```

```python
import functools
import math

import jax
import jax.numpy as jnp
from jax import lax
from jax.experimental import pallas as pl
from jax.experimental.pallas import tpu as pltpu

F32 = jnp.float32
BF16 = jnp.bfloat16

DEPTH = 4
N_MIXERS = 2
GRID_W = 64
DA_HEADS = 8
GW_HEADS = 32
GW_KV_HEADS = 4
GW_GROUP = GW_HEADS // GW_KV_HEADS
WINDOW = 128
N_GROUPS = 4
EXPERTS_PER_GROUP = 8
N_EXPERTS = N_GROUPS * EXPERTS_PER_GROUP
TOP_K = 2
ROPE_BASE = 10000.0
RMS_EPS = 1e-6
NEG_INF = -1e30
LOG2E = math.log2(math.e)

LANES = 128
COND_ROWS = 32
N_MOD = 6
ROUTER_LANES = 128
VMEM_LIMIT = 56 * 1024 * 1024


def _cparams(n_axes, vmem=VMEM_LIMIT):
    return pltpu.CompilerParams(dimension_semantics=("arbitrary",) * n_axes, vmem_limit_bytes=vmem)


def _dot(a, b):
    return jnp.dot(a, b, preferred_element_type=F32)


def _dot_nt(a, b):
    return lax.dot_general(a, b, (((1,), (1,)), ((), ())), preferred_element_type=F32)


def _ada_kernel(c_ref, w_ref, b_ref, o_ref):
    c = c_ref[...]
    s = (c / (1.0 + jnp.exp(-c))).astype(BF16)
    o_ref[...] = _dot(s, w_ref[...].astype(BF16)) + b_ref[...]


def _ada_call(cond, ada_w, ada_b):
    depth, d, n = ada_w.shape
    tn = 1024
    return pl.pallas_call(
        _ada_kernel,
        grid=(depth, n // tn),
        in_specs=[
            pl.BlockSpec((COND_ROWS, d), lambda l, j: (0, 0)),
            pl.BlockSpec((None, d, tn), lambda l, j: (l, 0, j)),
            pl.BlockSpec((None, 1, tn), lambda l, j: (l, 0, j)),
        ],
        out_specs=pl.BlockSpec((None, COND_ROWS, tn), lambda l, j: (l, 0, j)),
        out_shape=jax.ShapeDtypeStruct((depth, COND_ROWS, n), F32),
        compiler_params=_cparams(2),
        name="ada_mod",
    )(cond, ada_w, ada_b.reshape(depth, 1, n))


def _rope_tables(seq, dim):
    rows = seq // GRID_W
    row = jnp.repeat(jnp.arange(rows, dtype=F32), GRID_W)
    col = jnp.tile(jnp.arange(GRID_W, dtype=F32), rows)
    half = dim // 2
    inv = ROPE_BASE ** (-jnp.arange(0, half, 2, dtype=F32) / half)
    ang_r = row[:, None] * inv[None]
    ang_c = col[:, None] * inv[None]
    cr, sr, cc, sc = jnp.cos(ang_r), jnp.sin(ang_r), jnp.cos(ang_c), jnp.sin(ang_c)
    z = jnp.zeros_like(sr)
    reps = LANES // dim
    cos_t = jnp.tile(jnp.concatenate([cr, cr, cc, cc], axis=1), (1, reps))
    sin_a = jnp.tile(jnp.concatenate([-sr, z, -sc, z], axis=1), (1, reps))
    sin_b = jnp.tile(jnp.concatenate([z, sr, z, sc], axis=1), (1, reps))
    return cos_t, sin_a, sin_b


def _qkv_kernel(x_ref, sh_ref, sc_ref, g_ref, w_ref, cos_ref, sa_ref, sb_ref, o_ref, h_scr, *,
                ncb, n_rope_full, partial_cols, nq, qscale, rope_q, tn):
    r = pl.program_id(0)
    j = pl.program_id(1)

    @pl.when(j == 0)
    def _():
        x = x_ref[...]
        ms = jnp.mean(x * x, axis=-1, keepdims=True)
        y = x * lax.rsqrt(ms + RMS_EPS) * g_ref[...]
        h_scr[...] = (y * (1.0 + sc_ref[0]) + sh_ref[0]).astype(BF16)

    acc = _dot(h_scr[...], w_ref[...])
    if qscale != 1.0:
        acc = acc * jnp.where(j < nq, jnp.float32(qscale), jnp.float32(1.0))
    is_lat = r >= ncb

    def store(ncols):
        for ci in range(tn // LANES):
            a = acc[:, ci * LANES:(ci + 1) * LANES]
            if ci * LANES < ncols:
                a = (a * cos_ref[...] + pltpu.roll(a, LANES - rope_q, 1) * sa_ref[...]
                     + pltpu.roll(a, rope_q, 1) * sb_ref[...])
            o_ref[:, ci * LANES:(ci + 1) * LANES] = a.astype(BF16)

    full = jnp.logical_and(is_lat, j < n_rope_full)
    if partial_cols:
        part = jnp.logical_and(is_lat, j == n_rope_full)
        plain = jnp.logical_not(jnp.logical_or(full, part))
        pl.when(part)(lambda: store(partial_cols))
    else:
        plain = jnp.logical_not(full)
    pl.when(full)(lambda: store(tn))
    pl.when(plain)(lambda: store(0))


def _qkv_call(xa, modr, gain, w, tables, *, layer, n_ctx_rows, seq, n_rope_cols, n_q_cols, qscale, rope_dim):
    t, d = xa.shape
    n = w.shape[1]
    tn = 512
    tm = min(1024, seq, n_ctx_rows)
    assert t % tm == 0 and n_ctx_rows % tm == 0 and seq % tm == 0 and n % tn == 0 and n_q_cols % tn == 0
    ncb = n_ctx_rows // tm
    bpb = seq // tm
    base = layer * COND_ROWS

    def seg(r):
        return jnp.where(r < ncb, COND_ROWS // 2, (r - ncb) // bpb)

    def pos(r):
        return jnp.where(r < ncb, 0, (r - ncb) % bpb)

    kern = functools.partial(
        _qkv_kernel, ncb=ncb, n_rope_full=n_rope_cols // tn, partial_cols=n_rope_cols % tn,
        nq=n_q_cols // tn, qscale=qscale, rope_q=rope_dim // 4, tn=tn)
    tab_spec = pl.BlockSpec((tm, LANES), lambda r, j: (pos(r), 0))
    return pl.pallas_call(
        kern,
        grid=(t // tm, n // tn),
        in_specs=[
            pl.BlockSpec((tm, d), lambda r, j: (r, 0)),
            pl.BlockSpec((1, 1, d), lambda r, j: ((base + seg(r)) * N_MOD + 0, 0, 0)),
            pl.BlockSpec((1, 1, d), lambda r, j: ((base + seg(r)) * N_MOD + 1, 0, 0)),
            pl.BlockSpec((1, d), lambda r, j: (0, 0)),
            pl.BlockSpec((d, tn), lambda r, j: (0, j)),
            tab_spec, tab_spec, tab_spec,
        ],
        out_specs=pl.BlockSpec((tm, tn), lambda r, j: (r, j)),
        out_shape=jax.ShapeDtypeStruct((t, n), BF16),
        scratch_shapes=[pltpu.VMEM((tm, d), BF16)],
        compiler_params=_cparams(2),
        name=f"qkv_proj_l{layer}",
    )(xa, modr, modr, gain.reshape(1, d), w, *tables)


def _da_body(q_ref, kc_ref, vc_ref, kx_ref, vx_ref, lq1, lk1, lq2, lk2, g_ref, o_ref, *, lam_init, has_x):
    dh = q_ref.shape[1] // 2
    lam = (jnp.exp(jnp.sum(lq1[...] * lk1[...], axis=-1, keepdims=True))
           - jnp.exp(jnp.sum(lq2[...] * lk2[...], axis=-1, keepdims=True)) + lam_init)

    def probs(lo):
        qc = q_ref[:, lo:lo + dh]
        s_c = _dot_nt(qc, kc_ref[:, lo:lo + dh])
        m = jnp.max(s_c, axis=-1, keepdims=True)
        if has_x:
            s_x = _dot_nt(qc, kx_ref[:, lo:lo + dh])
            m = jnp.maximum(m, jnp.max(s_x, axis=-1, keepdims=True))
        e_c = jnp.exp2(s_c - m)
        den = jnp.sum(e_c, axis=-1, keepdims=True)
        e_x = None
        if has_x:
            e_x = jnp.exp2(s_x - m)
            den = den + jnp.sum(e_x, axis=-1, keepdims=True)
        return e_c, e_x, 1.0 / den

    e1c, e1x, r1 = probs(0)
    e2c, e2x, r2 = probs(dh)
    w2 = lam * r2
    o = _dot((e1c * r1 - e2c * w2).astype(BF16), vc_ref[...])
    if has_x:
        o = o + _dot((e1x * r1 - e2x * w2).astype(BF16), vx_ref[...])
    ms = jnp.mean(o * o, axis=-1, keepdims=True)
    o = o * lax.rsqrt(ms + RMS_EPS) * g_ref[...] * (1.0 - lam_init)
    o_ref[...] = o.astype(BF16)


def _da_kernel(*refs, lam_init, nqb, has_ctx_steps):
    body = functools.partial(_da_body, *refs, lam_init=lam_init)
    if has_ctx_steps:
        qb = pl.program_id(2)
        pl.when(qb < nqb)(lambda: body(has_x=True))
        pl.when(qb >= nqb)(lambda: body(has_x=False))
    else:
        body(has_x=True)


def _da_attention(qkv, lam_vecs, subln_g, *, layer, batch, seq, n_ctx, need_ctx):
    t, n3 = qkv.shape
    d = n3 // 3
    hw = d // DA_HEADS
    tc_rows = batch * n_ctx
    tq = min(256, seq)
    assert seq % tq == 0 and n_ctx % tq == 0 and tc_rows % seq == 0 and hw % LANES == 0
    lam_init = 0.8 - 0.6 * math.exp(-0.3 * layer)
    nh = DA_HEADS
    small = [v.reshape(1, -1) for v in lam_vecs] + [subln_g.reshape(1, hw)]
    small_specs = [pl.BlockSpec(v.shape, lambda b, h, qb: (0, 0)) for v in small]
    qoff = tc_rows // tq
    nqb = seq // tq
    ncq = n_ctx // tq if need_ctx else 0
    xoff = tc_rows // seq

    def q_row(b, qb):
        lat = qoff + b * nqb + jnp.minimum(qb, nqb - 1)
        return jnp.where(qb < nqb, lat, b * ncq + (qb - nqb)) if need_ctx else lat

    def o_row(b, qb):
        return q_row(b, qb) if need_ctx else b * nqb + qb

    return pl.pallas_call(
        functools.partial(_da_kernel, lam_init=lam_init, nqb=nqb, has_ctx_steps=need_ctx),
        grid=(batch, nh, nqb + ncq),
        in_specs=[
            pl.BlockSpec((tq, hw), lambda b, h, qb: (q_row(b, qb), h)),
            pl.BlockSpec((n_ctx, hw), lambda b, h, qb: (b, nh + h)),
            pl.BlockSpec((n_ctx, hw), lambda b, h, qb: (b, 2 * nh + h)),
            pl.BlockSpec((seq, hw), lambda b, h, qb: (xoff + b, nh + h)),
            pl.BlockSpec((seq, hw), lambda b, h, qb: (xoff + b, 2 * nh + h)),
        ] + small_specs,
        out_specs=pl.BlockSpec((tq, hw), lambda b, h, qb: (o_row(b, qb), h)),
        out_shape=jax.ShapeDtypeStruct((t if need_ctx else batch * seq, d), BF16),
        compiler_params=_cparams(3),
        name=f"diff_attn_l{layer}",
    )(qkv, qkv, qkv, qkv, qkv, *small)


def _gw_body(sink_ref, q_ref, kc_ref, vc_ref, kx_ref, vx_ref, o_ref, *, has_x, heads_per_step, group, dim, nqb):
    gp = pl.program_id(1)
    tq = q_ref.shape[0]
    half = LANES // 2
    assert dim == half
    tiles = group * dim // LANES
    kv_heads = heads_per_step // group
    if has_x:
        qb = pl.program_id(2)
        st_p = pl.multiple_of(jnp.maximum(qb - 1, 0) * tq, tq)
        st_m = pl.multiple_of(qb * tq, tq)
        st_n = pl.multiple_of(jnp.minimum(qb + 1, nqb - 1) * tq, tq)
        k_all = jnp.concatenate([kc_ref[...]] + [kx_ref[pl.ds(s, tq), :] for s in (st_p, st_m, st_n)], axis=0)
        v_all = jnp.concatenate([vc_ref[...]] + [vx_ref[pl.ds(s, tq), :] for s in (st_p, st_m, st_n)], axis=0)
        n_ctx = kc_ref.shape[0]
        r = lax.broadcasted_iota(jnp.int32, (tq, tq), 0)
        c = lax.broadcasted_iota(jnp.int32, (tq, tq), 1)
        zero = jnp.zeros((tq, tq), F32)
        bias_p = jnp.where(jnp.logical_and(c >= r, qb >= 1), 0.0, NEG_INF)
        bias_n = jnp.where(jnp.logical_and(c <= r, qb + 1 < nqb), 0.0, NEG_INF)
        bias = jnp.concatenate([jnp.zeros((tq, n_ctx), F32), bias_p, zero, bias_n], axis=1)
        bias = jnp.concatenate([bias] * tiles, axis=0)
    else:
        k_all = kc_ref[...]
        v_all = vc_ref[...]
        bias = None
    lo = lax.broadcasted_iota(jnp.int32, k_all.shape, 1) < half

    def half_variants(x):
        xf = x.astype(F32)
        xs = pltpu.roll(xf, half, 1)
        keep_lo = lambda a: jnp.where(lo, a, 0.0).astype(BF16)
        keep_hi = lambda a: jnp.where(lo, 0.0, a).astype(BF16)
        return [(keep_lo(xf), keep_hi(xs)), (keep_lo(xs), keep_hi(xf))]

    k_var = half_variants(k_all)
    v_var = half_variants(v_all)
    for gi in range(kv_heads):
        kz = k_var[gi]
        vz = v_var[gi]
        q_stack = jnp.concatenate(
            [q_ref[:, (gi * tiles + j) * LANES:(gi * tiles + j + 1) * LANES] for j in range(tiles)], axis=0)
        out = None
        for par in range(2):
            s = _dot_nt(q_stack, kz[par])
            if bias is not None:
                s = s + bias
            head0 = gp * heads_per_step + gi * group + par
            sk = jnp.concatenate(
                [jnp.full((tq, 1), sink_ref[head0 + 2 * j] * LOG2E, F32) for j in range(tiles)], axis=0)
            m = jnp.maximum(jnp.max(s, axis=-1, keepdims=True), sk)
            e = jnp.exp2(s - m)
            den = jnp.sum(e, axis=-1, keepdims=True) + jnp.exp2(sk - m)
            o = _dot(e.astype(BF16), vz[par]) * (1.0 / den)
            out = o if out is None else out + o
        for j in range(tiles):
            o_ref[:, (gi * tiles + j) * LANES:(gi * tiles + j + 1) * LANES] = out[j * tq:(j + 1) * tq].astype(BF16)


def _gw_kernel(*refs, nqb, has_ctx_steps, **kw):
    body = functools.partial(_gw_body, *refs, nqb=nqb, **kw)
    if has_ctx_steps:
        qb = pl.program_id(2)
        pl.when(qb < nqb)(lambda: body(has_x=True))
        pl.when(qb >= nqb)(lambda: body(has_x=False))
    else:
        body(has_x=True)


def _gw_attention(qkv, sinks, *, layer, batch, seq, n_ctx, need_ctx, d):
    t = qkv.shape[0]
    dim = d // GW_HEADS
    kv_per_step = LANES // dim
    hps = kv_per_step * GW_GROUP
    qw = hps * dim
    n_gp = GW_KV_HEADS // kv_per_step
    tq = WINDOW
    tc_rows = batch * n_ctx
    assert seq % tq == 0 and n_ctx % tq == 0 and tc_rows % seq == 0
    nqb = seq // tq
    ncq = n_ctx // tq if need_ctx else 0
    qoff = tc_rows // tq
    xoff = tc_rows // seq
    kcol = d // LANES
    vcol = kcol + GW_KV_HEADS * dim // LANES

    def q_row(b, qb):
        lat = qoff + b * nqb + jnp.minimum(qb, nqb - 1)
        return jnp.where(qb < nqb, lat, b * ncq + (qb - nqb)) if need_ctx else lat

    def o_row(b, qb):
        return q_row(b, qb) if need_ctx else b * nqb + qb

    return pl.pallas_call(
        functools.partial(_gw_kernel, nqb=nqb, has_ctx_steps=need_ctx, heads_per_step=hps, group=GW_GROUP, dim=dim),
        grid=(batch, n_gp, nqb + ncq),
        in_specs=[
            pl.BlockSpec(memory_space=pltpu.SMEM),
            pl.BlockSpec((tq, qw), lambda b, g, qb: (q_row(b, qb), g)),
            pl.BlockSpec((n_ctx, LANES), lambda b, g, qb: (b, kcol + g)),
            pl.BlockSpec((n_ctx, LANES), lambda b, g, qb: (b, vcol + g)),
            pl.BlockSpec((seq, LANES), lambda b, g, qb: (xoff + b, kcol + g)),
            pl.BlockSpec((seq, LANES), lambda b, g, qb: (xoff + b, vcol + g)),
        ],
        out_specs=pl.BlockSpec((tq, qw), lambda b, g, qb: (o_row(b, qb), g)),
        out_shape=jax.ShapeDtypeStruct((t if need_ctx else batch * seq, d), BF16),
        compiler_params=_cparams(3),
        name=f"gqa_attn_l{layer}",
    )(sinks, qkv, qkv, qkv, qkv, qkv)


def _pack_bf16_pairs(h):
    half = h.shape[1] // 2
    hi = lax.bitcast_convert_type(h[:, :half].astype(BF16).astype(F32), jnp.uint32)
    lo = lax.bitcast_convert_type(h[:, half:].astype(BF16).astype(F32), jnp.uint32)
    return hi | (lo >> 16)


def _unpack_bf16_pairs(u):
    hi = lax.bitcast_convert_type(u & jnp.uint32(0xFFFF0000), F32)
    lo = lax.bitcast_convert_type(u << 16, F32)
    return jnp.concatenate([hi, lo], axis=1).astype(BF16)


def _oproj_kernel(o_ref, w_ref, x_ref, g1_ref, sh_ref, sc_ref, gn_ref, rhi_ref, rlo_ref,
                  xo_ref, h_ref, lg_ref):
    xn = x_ref[...] + g1_ref[0] * _dot(o_ref[...], w_ref[...])
    xo_ref[...] = xn
    ms = jnp.mean(xn * xn, axis=-1, keepdims=True)
    h = xn * lax.rsqrt(ms + RMS_EPS) * gn_ref[...]
    h = h * (1.0 + sc_ref[0]) + sh_ref[0]
    h_ref[...] = _pack_bf16_pairs(h)
    h_hi = h.astype(BF16)
    h_lo = (h - h_hi.astype(F32)).astype(BF16)
    lg_ref[...] = _dot(h_hi, rhi_ref[...]) + (_dot(h_hi, rlo_ref[...]) + _dot(h_lo, rhi_ref[...]))


def _oproj_call(o, w, xa, modr, gain, r_hi, r_lo, *, layer, n_ctx_rows, seq, latent_only):
    t, d = xa.shape
    tm = 256
    assert n_ctx_rows % tm == 0 and seq % tm == 0
    ncb = n_ctx_rows // tm
    bpb = seq // tm
    off = ncb if latent_only else 0
    nblk = t // tm - off
    base = layer * COND_ROWS

    def mrow(r, chunk):
        blk = r + off
        seg = jnp.where(blk < ncb, COND_ROWS // 2, (blk - ncb) // bpb)
        return ((base + seg) * N_MOD + chunk, 0, 0)

    const = lambda r: (0, 0)
    return pl.pallas_call(
        _oproj_kernel,
        grid=(nblk,),
        in_specs=[
            pl.BlockSpec((tm, d), lambda r: (r, 0)),
            pl.BlockSpec((d, d), const, pipeline_mode=pl.Buffered(1)),
            pl.BlockSpec((tm, d), lambda r: (r + off, 0)),
            pl.BlockSpec((1, 1, d), lambda r: mrow(r, 2)),
            pl.BlockSpec((1, 1, d), lambda r: mrow(r, 3)),
            pl.BlockSpec((1, 1, d), lambda r: mrow(r, 4)),
            pl.BlockSpec((1, d), const),
            pl.BlockSpec((d, ROUTER_LANES), const),
            pl.BlockSpec((d, ROUTER_LANES), const),
        ],
        out_specs=[
            pl.BlockSpec((tm, d), lambda r: (r + off, 0)),
            pl.BlockSpec((tm, d // 2), lambda r: (r, 0)),
            pl.BlockSpec((tm, ROUTER_LANES), lambda r: (r, 0)),
        ],
        out_shape=[
            jax.ShapeDtypeStruct((t, d), F32),
            jax.ShapeDtypeStruct((nblk * tm, d // 2), jnp.uint32),
            jax.ShapeDtypeStruct((nblk * tm, ROUTER_LANES), F32),
        ],
        input_output_aliases={2: 0},
        compiler_params=_cparams(1),
        name=f"out_proj_l{layer}",
    )(o, w, xa, modr, modr, modr, gain.reshape(1, d), r_hi, r_lo)


def _inclusive_cumsum_rows(m):
    n, k = m.shape
    blk = LANES
    assert n % blk == 0
    tri = (jnp.arange(blk)[:, None] >= jnp.arange(blk)[None, :]).astype(F32)
    m3 = m.reshape(n // blk, blk, k)
    within = jnp.einsum("ij,bjk->bik", tri, m3, precision=lax.Precision.HIGHEST)
    sums = within[:, -1, :]
    nb = n // blk
    tri_b = (jnp.arange(nb)[:, None] > jnp.arange(nb)[None, :]).astype(F32)
    offs = jnp.dot(tri_b, sums, precision=lax.Precision.HIGHEST)
    return (within + offs[:, None, :]).reshape(n, k)


def _route(logits, tb):
    t = logits.shape[0]
    gl = logits[:, :N_GROUPS]
    g_sel = jnp.argmax(gl, axis=-1)
    g_prob = jax.nn.softmax(gl, axis=-1)
    g_w = jnp.take_along_axis(g_prob, g_sel[:, None], axis=1)[:, 0]
    el = logits[:, N_GROUPS:N_GROUPS + N_EXPERTS].reshape(t, N_GROUPS, EXPERTS_PER_GROUP)
    e_sel = jnp.take_along_axis(el, g_sel[:, None, None], axis=1)[:, 0]
    e_prob = jax.nn.softmax(e_sel, axis=-1)
    lane = jnp.arange(EXPERTS_PER_GROUP, dtype=jnp.int32)[None]
    i1 = jnp.argmax(e_prob, axis=-1).astype(jnp.int32)
    p1 = jnp.max(e_prob, axis=-1)
    rest = jnp.where(lane == i1[:, None], -1.0, e_prob)
    i2 = jnp.argmax(rest, axis=-1).astype(jnp.int32)
    p2 = jnp.max(rest, axis=-1)
    top_p = jnp.stack([p1, p2], axis=-1)
    top_i = jnp.stack([i1, i2], axis=-1)
    top_p = top_p / jnp.sum(top_p, axis=-1, keepdims=True)
    w = g_w[:, None] * top_p
    eid = (g_sel[:, None].astype(jnp.int32) * EXPERTS_PER_GROUP + top_i).reshape(-1)
    a = t * TOP_K
    onehot = (eid[:, None] == jnp.arange(N_EXPERTS, dtype=jnp.int32)[None]).astype(F32)
    csum = _inclusive_cumsum_rows(onehot)
    counts = csum[-1].astype(jnp.int32)
    nblk = (counts + tb - 1) // tb
    bend = jnp.cumsum(nblk)
    bstart = bend - nblk
    slot = jnp.sum(onehot * (csum - 1.0 + (bstart * tb).astype(F32)[None]), axis=-1)
    dest = slot.astype(jnp.int32)
    n_blocks = -(-a // tb) + N_EXPERTS
    block_expert = jnp.clip(jnp.searchsorted(bend, jnp.arange(n_blocks, dtype=jnp.int32), side="right"),
                            0, N_EXPERTS - 1).astype(jnp.int32)
    nact = bend[-1:].astype(jnp.int32)
    return w, dest, block_expert, nact, n_blocks


def _dispatch_kernel(dest_ref, h_hbm, zero_hbm, xs_hbm, sem):
    del zero_hbm
    s = pl.program_id(0)
    n_rows = dest_ref.shape[2]
    tt = n_rows // TOP_K
    slot = s % 2

    def body(r, carry):
        src = h_hbm.at[pl.ds(s * tt + r, 1)]
        for k in range(TOP_K):
            pltpu.make_async_copy(src, xs_hbm.at[pl.ds(dest_ref[0, 0, TOP_K * r + k], 1)], sem.at[slot]).start()
        return carry
    lax.fori_loop(0, tt, body, 0, unroll=8)

    def wait_batch(sl):
        pltpu.make_async_copy(h_hbm.at[pl.ds(0, n_rows)], xs_hbm.at[pl.ds(0, n_rows)], sem.at[sl]).wait()

    pl.when(s >= 1)(lambda: wait_batch(1 - slot))
    pl.when(s == pl.num_programs(0) - 1)(lambda: wait_batch(slot))


def _dispatch_call(hp, dest, *, n_slots, layer):
    t, dw = hp.shape
    tt = 256
    assert t % tt == 0
    any_spec = pl.BlockSpec(memory_space=pl.ANY)
    return pl.pallas_call(
        _dispatch_kernel,
        grid=(t // tt,),
        in_specs=[
            pl.BlockSpec((1, 1, TOP_K * tt), lambda s: (s, 0, 0), memory_space=pltpu.SMEM),
            any_spec,
            any_spec,
        ],
        out_specs=any_spec,
        out_shape=jax.ShapeDtypeStruct((n_slots, dw), hp.dtype),
        scratch_shapes=[pltpu.SemaphoreType.DMA((2,))],
        input_output_aliases={2: 0},
        compiler_params=_cparams(1),
        name=f"moe_dispatch_l{layer}",
    )(dest.reshape(t // tt, 1, TOP_K * tt), hp, jnp.zeros((n_slots, dw), hp.dtype))


def _moe_kernel(be_ref, nact_ref, xs_ref, wg_ref, wu_ref, wd_ref, y_ref, wg_s, wu_s, wd_s):
    i = pl.program_id(0)
    active = i < nact_ref[0]
    fresh = jnp.logical_or(i == 0, be_ref[i] != be_ref[jnp.maximum(i - 1, 0)])

    @pl.when(jnp.logical_and(active, fresh))
    def _():
        wg_s[...] = wg_ref[...].astype(BF16)
        wu_s[...] = wu_ref[...].astype(BF16)
        wd_s[...] = wd_ref[...].astype(BF16)

    @pl.when(active)
    def _():
        xb = _unpack_bf16_pairs(xs_ref[...])
        g = _dot(xb, wg_s[...])
        u = _dot(xb, wu_s[...])
        act = (g / (1.0 + jnp.exp(-g))) * u
        y_ref[...] = _dot(act.astype(BF16), wd_s[...])

    @pl.when(jnp.logical_not(active))
    def _():
        y_ref[...] = jnp.zeros_like(y_ref)


def _moe_call(xs, block_expert, nact, wg, wu, wd, *, n_blocks, tb, layer):
    p, dw = xs.shape
    d, de = wg.shape[1], wg.shape[2]
    assert p == n_blocks * tb
    grid_spec = pltpu.PrefetchScalarGridSpec(
        num_scalar_prefetch=2,
        grid=(n_blocks,),
        in_specs=[
            pl.BlockSpec((tb, dw), lambda i, be, na: (jnp.minimum(i, na[0] - 1), 0)),
            pl.BlockSpec((None, d, de), lambda i, be, na: (be[i], 0, 0)),
            pl.BlockSpec((None, d, de), lambda i, be, na: (be[i], 0, 0)),
            pl.BlockSpec((None, de, d), lambda i, be, na: (be[i], 0, 0)),
        ],
        out_specs=pl.BlockSpec((tb, d), lambda i, be, na: (i, 0)),
        scratch_shapes=[pltpu.VMEM((d, de), BF16), pltpu.VMEM((d, de), BF16), pltpu.VMEM((de, d), BF16)],
    )
    return pl.pallas_call(
        _moe_kernel,
        grid_spec=grid_spec,
        out_shape=jax.ShapeDtypeStruct((p, d), F32),
        compiler_params=_cparams(1),
        name=f"moe_experts_l{layer}",
    )(block_expert, nact, xs, wg, wu, wd)


def _row_gather_start(idx_ref, n_rows, src_hbm, dst, sem, *, idx_stride=1, idx_off=0):
    def body(r, carry):
        tok = idx_ref[0, 0, r * idx_stride + idx_off]
        pltpu.make_async_copy(src_hbm.at[pl.ds(tok, 1)], dst.at[pl.ds(r, 1)], sem).start()
        return carry
    lax.fori_loop(0, n_rows, body, 0, unroll=8)


def _row_gather_wait(n_rows, src_hbm, dst, sem):
    pltpu.make_async_copy(src_hbm.at[pl.ds(0, n_rows)], dst, sem).wait()


def _combine_kernel(pos_ref, y_hbm, w_ref, x_ref, g2_ref, gout_ref, o_ref, ybuf, sem, *, final_norm):
    s = pl.program_id(0)
    nb = pl.num_programs(0) - 1
    tc = ybuf.shape[2]

    @pl.when(s < nb)
    def _():
        slot = s % 2
        for k in range(TOP_K):
            _row_gather_start(pos_ref, tc, y_hbm, ybuf.at[slot, k], sem.at[slot], idx_stride=TOP_K, idx_off=k)

    @pl.when(s >= 1)
    def _():
        slot = (s - 1) % 2
        for k in range(TOP_K):
            _row_gather_wait(tc, y_hbm, ybuf.at[slot, k], sem.at[slot])
        w = w_ref[...]
        moe = ybuf[slot, 0] * w[:, 0:1] + ybuf[slot, 1] * w[:, 1:2]
        xn = x_ref[...] + g2_ref[0] * moe
        if final_norm:
            ms = jnp.mean(xn * xn, axis=-1, keepdims=True)
            xn = xn * lax.rsqrt(ms + RMS_EPS) * gout_ref[...]
        o_ref[...] = xn


def _combine_call(y, dest, w, xa, modr, gout, *, layer, n_ctx_rows, seq, latent_only, final_norm):
    t, d = xa.shape
    tc = 256
    assert n_ctx_rows % tc == 0 and seq % tc == 0
    ncb = n_ctx_rows // tc
    bpb = seq // tc
    off = ncb if latent_only else 0
    nblk = t // tc - off
    base = layer * COND_ROWS
    prev = lambda s: jnp.maximum(s - 1, 0)

    def mrow(s):
        blk = prev(s) + off
        seg = jnp.where(blk < ncb, COND_ROWS // 2, (blk - ncb) // bpb)
        return ((base + seg) * N_MOD + 5, 0, 0)

    out_off = 0 if final_norm else off
    out_rows = nblk * tc if final_norm else t
    return pl.pallas_call(
        functools.partial(_combine_kernel, final_norm=final_norm),
        grid=(nblk + 1,),
        in_specs=[
            pl.BlockSpec((1, 1, TOP_K * tc), lambda s: (jnp.minimum(s, nblk - 1), 0, 0), memory_space=pltpu.SMEM),
            pl.BlockSpec(memory_space=pl.ANY),
            pl.BlockSpec((tc, TOP_K), lambda s: (prev(s), 0)),
            pl.BlockSpec((tc, d), lambda s: (prev(s) + off, 0)),
            pl.BlockSpec((1, 1, d), mrow),
            pl.BlockSpec((1, d), lambda s: (0, 0)),
        ],
        out_specs=pl.BlockSpec((tc, d), lambda s: (prev(s) + out_off, 0)),
        out_shape=jax.ShapeDtypeStruct((out_rows, d), F32),
        scratch_shapes=[pltpu.VMEM((2, TOP_K, tc, d), F32), pltpu.SemaphoreType.DMA((2,))],
        input_output_aliases={} if final_norm else {3: 0},
        compiler_params=_cparams(1),
        name=f"moe_combine_l{layer}",
    )(dest.reshape(nblk, 1, TOP_K * tc), y, w, xa, modr, gout.reshape(1, d))


def kernel(x, c, ctx, c_ctx, ada_w, ada_b, norm_mix_g, norm_ffn_g, norm_out_g, da_w_in, da_w_out,
           da_lam_q1, da_lam_k1, da_lam_q2, da_lam_k2, da_subln_g, gw_w_in, gw_w_out, gw_sinks,
           moe_w_group, moe_w_expert, moe_w_gate, moe_w_up, moe_w_down):
    batch, seq, d = x.shape
    n_ctx = ctx.shape[1]
    tc_rows = batch * n_ctx
    depth = ada_w.shape[0]
    assert batch < COND_ROWS // 2 + 1 and depth == DEPTH

    xa = jnp.concatenate([ctx.reshape(tc_rows, d), x.reshape(batch * seq, d)], axis=0)
    cond = jnp.zeros((COND_ROWS, d), F32).at[:batch].set(c).at[COND_ROWS // 2].set(c_ctx)
    mod = _ada_call(cond, ada_w, ada_b)
    modr = mod.reshape(depth * COND_ROWS * N_MOD, 1, d)

    da_dim = d // (2 * DA_HEADS)
    gw_dim = d // GW_HEADS
    tables_a = _rope_tables(seq, da_dim)
    tables_b = _rope_tables(seq, gw_dim)
    tb = 256
    out = None
    for i in range(depth):
        last = i == depth - 1
        need_ctx = not last
        jm = i // N_MIXERS
        if i % N_MIXERS == 0:
            qkv = _qkv_call(xa, modr, norm_mix_g[i], da_w_in[jm].astype(BF16), tables_a, layer=i,
                            n_ctx_rows=tc_rows, seq=seq, n_rope_cols=2 * d, n_q_cols=d,
                            qscale=da_dim ** -0.5 * LOG2E, rope_dim=da_dim)
            o = _da_attention(qkv, (da_lam_q1[jm], da_lam_k1[jm], da_lam_q2[jm], da_lam_k2[jm]),
                              da_subln_g[jm], layer=i, batch=batch, seq=seq, n_ctx=n_ctx, need_ctx=need_ctx)
            w_out = da_w_out[jm]
        else:
            kv_w = GW_KV_HEADS * gw_dim
            qkv = _qkv_call(xa, modr, norm_mix_g[i], gw_w_in[jm].astype(BF16), tables_b, layer=i,
                            n_ctx_rows=tc_rows, seq=seq, n_rope_cols=d + kv_w, n_q_cols=d,
                            qscale=gw_dim ** -0.5 * LOG2E, rope_dim=gw_dim)
            o = _gw_attention(qkv, gw_sinks[jm], layer=i, batch=batch, seq=seq, n_ctx=n_ctx,
                              need_ctx=need_ctx, d=d)
            w_out = gw_w_out[jm]
        w_r = jnp.concatenate([moe_w_group[i], moe_w_expert[i]], axis=1)
        w_r = jnp.pad(w_r, ((0, 0), (0, ROUTER_LANES - w_r.shape[1])))
        r_hi = w_r.astype(BF16)
        r_lo = (w_r - r_hi.astype(F32)).astype(BF16)
        xa, h2, logits = _oproj_call(o, w_out.astype(BF16), xa, modr, norm_ffn_g[i], r_hi, r_lo, layer=i,
                                     n_ctx_rows=tc_rows, seq=seq, latent_only=last)
        w_tok, dest, block_expert, nact, n_blocks = _route(logits, tb)
        xs = _dispatch_call(h2, dest, n_slots=n_blocks * tb, layer=i)
        y = _moe_call(xs, block_expert, nact, moe_w_gate[i], moe_w_up[i], moe_w_down[i],
                      n_blocks=n_blocks, tb=tb, layer=i)
        res = _combine_call(y, dest, w_tok, xa, modr, norm_out_g, layer=i, n_ctx_rows=tc_rows, seq=seq,
                            latent_only=last, final_norm=last)
        if last:
            out = res
        else:
            xa = res
    return out.reshape(batch, seq, d)
```

```python
import functools
import math

import jax
import jax.numpy as jnp
from jax import lax
from jax.experimental import pallas as pl
from jax.experimental.pallas import tpu as pltpu

F32 = jnp.float32
BF16 = jnp.bfloat16

DEPTH = 4
N_MIXERS = 2
GRID_W = 64
DA_HEADS = 8
GW_HEADS = 32
GW_KV_HEADS = 4
GW_GROUP = GW_HEADS // GW_KV_HEADS
WINDOW = 128
N_GROUPS = 4
EXPERTS_PER_GROUP = 8
N_EXPERTS = N_GROUPS * EXPERTS_PER_GROUP
TOP_K = 2
ROPE_BASE = 10000.0
RMS_EPS = 1e-6
NEG_INF = -1e30
LOG2E = math.log2(math.e)

LANES = 128
COND_ROWS = 32
N_MOD = 6
ROUTER_LANES = 128
VMEM_LIMIT = 56 * 1024 * 1024


def _cparams(n_axes, vmem=VMEM_LIMIT):
    return pltpu.CompilerParams(dimension_semantics=("arbitrary",) * n_axes, vmem_limit_bytes=vmem)


def _dot(a, b):
    return jnp.dot(a, b, preferred_element_type=F32)


def _dot_nt(a, b):
    return lax.dot_general(a, b, (((1,), (1,)), ((), ())), preferred_element_type=F32)


def _ada_kernel(c_ref, w_ref, b_ref, o_ref):
    c = c_ref[...]
    s = (c / (1.0 + jnp.exp(-c))).astype(BF16)
    o_ref[...] = _dot(s, w_ref[...].astype(BF16)) + b_ref[...]


def _ada_call(cond, ada_w, ada_b):
    depth, d, n = ada_w.shape
    tn = 1024
    return pl.pallas_call(
        _ada_kernel,
        grid=(depth, n // tn),
        in_specs=[
            pl.BlockSpec((COND_ROWS, d), lambda l, j: (0, 0)),
            pl.BlockSpec((None, d, tn), lambda l, j: (l, 0, j)),
            pl.BlockSpec((None, 1, tn), lambda l, j: (l, 0, j)),
        ],
        out_specs=pl.BlockSpec((None, COND_ROWS, tn), lambda l, j: (l, 0, j)),
        out_shape=jax.ShapeDtypeStruct((depth, COND_ROWS, n), F32),
        compiler_params=_cparams(2),
        name="ada_mod",
    )(cond, ada_w, ada_b.reshape(depth, 1, n))


def _rope_tables(seq, dim):
    rows = seq // GRID_W
    row = jnp.repeat(jnp.arange(rows, dtype=F32), GRID_W)
    col = jnp.tile(jnp.arange(GRID_W, dtype=F32), rows)
    half = dim // 2
    inv = ROPE_BASE ** (-jnp.arange(0, half, 2, dtype=F32) / half)
    ang_r = row[:, None] * inv[None]
    ang_c = col[:, None] * inv[None]
    cr, sr, cc, sc = jnp.cos(ang_r), jnp.sin(ang_r), jnp.cos(ang_c), jnp.sin(ang_c)
    z = jnp.zeros_like(sr)
    reps = LANES // dim
    cos_t = jnp.tile(jnp.concatenate([cr, cr, cc, cc], axis=1), (1, reps))
    sin_a = jnp.tile(jnp.concatenate([-sr, z, -sc, z], axis=1), (1, reps))
    sin_b = jnp.tile(jnp.concatenate([z, sr, z, sc], axis=1), (1, reps))
    return cos_t, sin_a, sin_b


def _qkv_kernel(x_ref, sh_ref, sc_ref, g_ref, w_ref, cos_ref, sa_ref, sb_ref, o_ref, h_scr, *,
                ncb, n_rope_full, partial_cols, nq, qscale, rope_q, tn):
    r = pl.program_id(0)
    j = pl.program_id(1)

    @pl.when(j == 0)
    def _():
        x = x_ref[...]
        ms = jnp.mean(x * x, axis=-1, keepdims=True)
        y = x * lax.rsqrt(ms + RMS_EPS) * g_ref[...]
        h_scr[...] = (y * (1.0 + sc_ref[0]) + sh_ref[0]).astype(BF16)

    acc = _dot(h_scr[...], w_ref[...])
    if qscale != 1.0:
        acc = acc * jnp.where(j < nq, jnp.float32(qscale), jnp.float32(1.0))
    is_lat = r >= ncb

    def store(ncols):
        for ci in range(tn // LANES):
            a = acc[:, ci * LANES:(ci + 1) * LANES]
            if ci * LANES < ncols:
                a = (a * cos_ref[...] + pltpu.roll(a, LANES - rope_q, 1) * sa_ref[...]
                     + pltpu.roll(a, rope_q, 1) * sb_ref[...])
            o_ref[:, ci * LANES:(ci + 1) * LANES] = a.astype(BF16)

    full = jnp.logical_and(is_lat, j < n_rope_full)
    if partial_cols:
        part = jnp.logical_and(is_lat, j == n_rope_full)
        plain = jnp.logical_not(jnp.logical_or(full, part))
        pl.when(part)(lambda: store(partial_cols))
    else:
        plain = jnp.logical_not(full)
    pl.when(full)(lambda: store(tn))
    pl.when(plain)(lambda: store(0))


def _qkv_call(xa, modr, gain, w, tables, *, layer, n_ctx_rows, seq, n_rope_cols, n_q_cols, qscale, rope_dim):
    t, d = xa.shape
    n = w.shape[1]
    tn = 512
    tm = min(1024, seq, n_ctx_rows)
    assert t % tm == 0 and n_ctx_rows % tm == 0 and seq % tm == 0 and n % tn == 0 and n_q_cols % tn == 0
    ncb = n_ctx_rows // tm
    bpb = seq // tm
    base = layer * COND_ROWS

    def seg(r):
        return jnp.where(r < ncb, COND_ROWS // 2, (r - ncb) // bpb)

    def pos(r):
        return jnp.where(r < ncb, 0, (r - ncb) % bpb)

    kern = functools.partial(
        _qkv_kernel, ncb=ncb, n_rope_full=n_rope_cols // tn, partial_cols=n_rope_cols % tn,
        nq=n_q_cols // tn, qscale=qscale, rope_q=rope_dim // 4, tn=tn)
    tab_spec = pl.BlockSpec((tm, LANES), lambda r, j: (pos(r), 0))
    return pl.pallas_call(
        kern,
        grid=(t // tm, n // tn),
        in_specs=[
            pl.BlockSpec((tm, d), lambda r, j: (r, 0)),
            pl.BlockSpec((1, 1, d), lambda r, j: ((base + seg(r)) * N_MOD + 0, 0, 0)),
            pl.BlockSpec((1, 1, d), lambda r, j: ((base + seg(r)) * N_MOD + 1, 0, 0)),
            pl.BlockSpec((1, d), lambda r, j: (0, 0)),
            pl.BlockSpec((d, tn), lambda r, j: (0, j)),
            tab_spec, tab_spec, tab_spec,
        ],
        out_specs=pl.BlockSpec((tm, tn), lambda r, j: (r, j)),
        out_shape=jax.ShapeDtypeStruct((t, n), BF16),
        scratch_shapes=[pltpu.VMEM((tm, d), BF16)],
        compiler_params=_cparams(2),
        name=f"qkv_proj_l{layer}",
    )(xa, modr, modr, gain.reshape(1, d), w, *tables)


def _da_body(q_ref, kc_ref, vc_ref, kx_ref, vx_ref, lq1, lk1, lq2, lk2, g_ref, o_ref, *, lam_init, has_x):
    dh = q_ref.shape[1] // 2
    lam = (jnp.exp(jnp.sum(lq1[...] * lk1[...], axis=-1, keepdims=True))
           - jnp.exp(jnp.sum(lq2[...] * lk2[...], axis=-1, keepdims=True)) + lam_init)

    def probs(lo):
        qc = q_ref[:, lo:lo + dh]
        s_c = _dot_nt(qc, kc_ref[:, lo:lo + dh])
        m = jnp.max(s_c, axis=-1, keepdims=True)
        if has_x:
            s_x = _dot_nt(qc, kx_ref[:, lo:lo + dh])
            m = jnp.maximum(m, jnp.max(s_x, axis=-1, keepdims=True))
        e_c = jnp.exp2(s_c - m)
        den = jnp.sum(e_c, axis=-1, keepdims=True)
        e_x = None
        if has_x:
            e_x = jnp.exp2(s_x - m)
            den = den + jnp.sum(e_x, axis=-1, keepdims=True)
        return e_c, e_x, 1.0 / den

    e1c, e1x, r1 = probs(0)
    e2c, e2x, r2 = probs(dh)
    w2 = lam * r2
    o = _dot((e1c * r1 - e2c * w2).astype(BF16), vc_ref[...])
    if has_x:
        o = o + _dot((e1x * r1 - e2x * w2).astype(BF16), vx_ref[...])
    ms = jnp.mean(o * o, axis=-1, keepdims=True)
    o = o * lax.rsqrt(ms + RMS_EPS) * g_ref[...] * (1.0 - lam_init)
    o_ref[...] = o.astype(BF16)


def _da_kernel(*refs, lam_init, nqb, has_ctx_steps):
    body = functools.partial(_da_body, *refs, lam_init=lam_init)
    if has_ctx_steps:
        qb = pl.program_id(2)
        pl.when(qb < nqb)(lambda: body(has_x=True))
        pl.when(qb >= nqb)(lambda: body(has_x=False))
    else:
        body(has_x=True)


def _da_attention(qkv, lam_vecs, subln_g, *, layer, batch, seq, n_ctx, need_ctx):
    t, n3 = qkv.shape
    d = n3 // 3
    hw = d // DA_HEADS
    tc_rows = batch * n_ctx
    tq = min(256, seq)
    assert seq % tq == 0 and n_ctx % tq == 0 and tc_rows % seq == 0 and hw % LANES == 0
    lam_init = 0.8 - 0.6 * math.exp(-0.3 * layer)
    nh = DA_HEADS
    small = [v.reshape(1, -1) for v in lam_vecs] + [subln_g.reshape(1, hw)]
    small_specs = [pl.BlockSpec(v.shape, lambda b, h, qb: (0, 0)) for v in small]
    qoff = tc_rows // tq
    nqb = seq // tq
    ncq = n_ctx // tq if need_ctx else 0
    xoff = tc_rows // seq

    def q_row(b, qb):
        lat = qoff + b * nqb + jnp.minimum(qb, nqb - 1)
        return jnp.where(qb < nqb, lat, b * ncq + (qb - nqb)) if need_ctx else lat

    def o_row(b, qb):
        return q_row(b, qb) if need_ctx else b * nqb + qb

    return pl.pallas_call(
        functools.partial(_da_kernel, lam_init=lam_init, nqb=nqb, has_ctx_steps=need_ctx),
        grid=(batch, nh, nqb + ncq),
        in_specs=[
            pl.BlockSpec((tq, hw), lambda b, h, qb: (q_row(b, qb), h)),
            pl.BlockSpec((n_ctx, hw), lambda b, h, qb: (b, nh + h)),
            pl.BlockSpec((n_ctx, hw), lambda b, h, qb: (b, 2 * nh + h)),
            pl.BlockSpec((seq, hw), lambda b, h, qb: (xoff + b, nh + h)),
            pl.BlockSpec((seq, hw), lambda b, h, qb: (xoff + b, 2 * nh + h)),
        ] + small_specs,
        out_specs=pl.BlockSpec((tq, hw), lambda b, h, qb: (o_row(b, qb), h)),
        out_shape=jax.ShapeDtypeStruct((t if need_ctx else batch * seq, d), BF16),
        compiler_params=_cparams(3),
        name=f"diff_attn_l{layer}",
    )(qkv, qkv, qkv, qkv, qkv, *small)


def _gw_body(sink_ref, q_ref, kc_ref, vc_ref, kx_ref, vx_ref, o_ref, *, has_x, heads_per_step, group, dim, nqb):
    gp = pl.program_id(1)
    tq = q_ref.shape[0]
    half = LANES // 2
    assert dim == half
    tiles = group * dim // LANES
    kv_heads = heads_per_step // group
    if has_x:
        qb = pl.program_id(2)
        st_p = pl.multiple_of(jnp.maximum(qb - 1, 0) * tq, tq)
        st_m = pl.multiple_of(qb * tq, tq)
        st_n = pl.multiple_of(jnp.minimum(qb + 1, nqb - 1) * tq, tq)
        k_all = jnp.concatenate([kc_ref[...]] + [kx_ref[pl.ds(s, tq), :] for s in (st_p, st_m, st_n)], axis=0)
        v_all = jnp.concatenate([vc_ref[...]] + [vx_ref[pl.ds(s, tq), :] for s in (st_p, st_m, st_n)], axis=0)
        n_ctx = kc_ref.shape[0]
        r = lax.broadcasted_iota(jnp.int32, (tq, tq), 0)
        c = lax.broadcasted_iota(jnp.int32, (tq, tq), 1)
        zero = jnp.zeros((tq, tq), F32)
        bias_p = jnp.where(jnp.logical_and(c >= r, qb >= 1), 0.0, NEG_INF)
        bias_n = jnp.where(jnp.logical_and(c <= r, qb + 1 < nqb), 0.0, NEG_INF)
        bias = jnp.concatenate([jnp.zeros((tq, n_ctx), F32), bias_p, zero, bias_n], axis=1)
        bias = jnp.concatenate([bias] * tiles, axis=0)
    else:
        k_all = kc_ref[...]
        v_all = vc_ref[...]
        bias = None
    lo = lax.broadcasted_iota(jnp.int32, k_all.shape, 1) < half

    def half_variants(x):
        xf = x.astype(F32)
        xs = pltpu.roll(xf, half, 1)
        keep_lo = lambda a: jnp.where(lo, a, 0.0).astype(BF16)
        keep_hi = lambda a: jnp.where(lo, 0.0, a).astype(BF16)
        return [(keep_lo(xf), keep_hi(xs)), (keep_lo(xs), keep_hi(xf))]

    k_var = half_variants(k_all)
    v_var = half_variants(v_all)
    for gi in range(kv_heads):
        kz = k_var[gi]
        vz = v_var[gi]
        q_stack = jnp.concatenate(
            [q_ref[:, (gi * tiles + j) * LANES:(gi * tiles + j + 1) * LANES] for j in range(tiles)], axis=0)
        out = None
        for par in range(2):
            s = _dot_nt(q_stack, kz[par])
            if bias is not None:
                s = s + bias
            head0 = gp * heads_per_step + gi * group + par
            sk = jnp.concatenate(
                [jnp.full((tq, 1), sink_ref[head0 + 2 * j] * LOG2E, F32) for j in range(tiles)], axis=0)
            m = jnp.maximum(jnp.max(s, axis=-1, keepdims=True), sk)
            e = jnp.exp2(s - m)
            den = jnp.sum(e, axis=-1, keepdims=True) + jnp.exp2(sk - m)
            o = _dot(e.astype(BF16), vz[par]) * (1.0 / den)
            out = o if out is None else out + o
        for j in range(tiles):
            o_ref[:, (gi * tiles + j) * LANES:(gi * tiles + j + 1) * LANES] = out[j * tq:(j + 1) * tq].astype(BF16)


def _gw_kernel(*refs, nqb, has_ctx_steps, **kw):
    body = functools.partial(_gw_body, *refs, nqb=nqb, **kw)
    if has_ctx_steps:
        qb = pl.program_id(2)
        pl.when(qb < nqb)(lambda: body(has_x=True))
        pl.when(qb >= nqb)(lambda: body(has_x=False))
    else:
        body(has_x=True)


def _gw_attention(qkv, sinks, *, layer, batch, seq, n_ctx, need_ctx, d):
    t = qkv.shape[0]
    dim = d // GW_HEADS
    kv_per_step = LANES // dim
    hps = kv_per_step * GW_GROUP
    qw = hps * dim
    n_gp = GW_KV_HEADS // kv_per_step
    tq = WINDOW
    tc_rows = batch * n_ctx
    assert seq % tq == 0 and n_ctx % tq == 0 and tc_rows % seq == 0
    nqb = seq // tq
    ncq = n_ctx // tq if need_ctx else 0
    qoff = tc_rows // tq
    xoff = tc_rows // seq
    kcol = d // LANES
    vcol = kcol + GW_KV_HEADS * dim // LANES

    def q_row(b, qb):
        lat = qoff + b * nqb + jnp.minimum(qb, nqb - 1)
        return jnp.where(qb < nqb, lat, b * ncq + (qb - nqb)) if need_ctx else lat

    def o_row(b, qb):
        return q_row(b, qb) if need_ctx else b * nqb + qb

    return pl.pallas_call(
        functools.partial(_gw_kernel, nqb=nqb, has_ctx_steps=need_ctx, heads_per_step=hps, group=GW_GROUP, dim=dim),
        grid=(batch, n_gp, nqb + ncq),
        in_specs=[
            pl.BlockSpec(memory_space=pltpu.SMEM),
            pl.BlockSpec((tq, qw), lambda b, g, qb: (q_row(b, qb), g)),
            pl.BlockSpec((n_ctx, LANES), lambda b, g, qb: (b, kcol + g)),
            pl.BlockSpec((n_ctx, LANES), lambda b, g, qb: (b, vcol + g)),
            pl.BlockSpec((seq, LANES), lambda b, g, qb: (xoff + b, kcol + g)),
            pl.BlockSpec((seq, LANES), lambda b, g, qb: (xoff + b, vcol + g)),
        ],
        out_specs=pl.BlockSpec((tq, qw), lambda b, g, qb: (o_row(b, qb), g)),
        out_shape=jax.ShapeDtypeStruct((t if need_ctx else batch * seq, d), BF16),
        compiler_params=_cparams(3),
        name=f"gqa_attn_l{layer}",
    )(sinks, qkv, qkv, qkv, qkv, qkv)


def _pack_bf16_pairs(h):
    half = h.shape[1] // 2
    hi = lax.bitcast_convert_type(h[:, :half].astype(BF16).astype(F32), jnp.uint32)
    lo = lax.bitcast_convert_type(h[:, half:].astype(BF16).astype(F32), jnp.uint32)
    return hi | (lo >> 16)


def _unpack_bf16_pairs(u):
    hi = lax.bitcast_convert_type(u & jnp.uint32(0xFFFF0000), F32)
    lo = lax.bitcast_convert_type(u << 16, F32)
    return jnp.concatenate([hi, lo], axis=1).astype(BF16)


def _oproj_kernel(o_ref, w_ref, x_ref, g1_ref, sh_ref, sc_ref, gn_ref, rhi_ref, rlo_ref,
                  xo_ref, h_ref, lg_ref):
    xn = x_ref[...] + g1_ref[0] * _dot(o_ref[...], w_ref[...])
    xo_ref[...] = xn
    ms = jnp.mean(xn * xn, axis=-1, keepdims=True)
    h = xn * lax.rsqrt(ms + RMS_EPS) * gn_ref[...]
    h = h * (1.0 + sc_ref[0]) + sh_ref[0]
    h_ref[...] = _pack_bf16_pairs(h)
    h_hi = h.astype(BF16)
    h_lo = (h - h_hi.astype(F32)).astype(BF16)
    lg_ref[...] = _dot(h_hi, rhi_ref[...]) + (_dot(h_hi, rlo_ref[...]) + _dot(h_lo, rhi_ref[...]))


def _oproj_call(o, w, xa, modr, gain, r_hi, r_lo, *, layer, n_ctx_rows, seq, latent_only):
    t, d = xa.shape
    tm = 256
    assert n_ctx_rows % tm == 0 and seq % tm == 0
    ncb = n_ctx_rows // tm
    bpb = seq // tm
    off = ncb if latent_only else 0
    nblk = t // tm - off
    base = layer * COND_ROWS

    def mrow(r, chunk):
        blk = r + off
        seg = jnp.where(blk < ncb, COND_ROWS // 2, (blk - ncb) // bpb)
        return ((base + seg) * N_MOD + chunk, 0, 0)

    const = lambda r: (0, 0)
    return pl.pallas_call(
        _oproj_kernel,
        grid=(nblk,),
        in_specs=[
            pl.BlockSpec((tm, d), lambda r: (r, 0)),
            pl.BlockSpec((d, d), const, pipeline_mode=pl.Buffered(1)),
            pl.BlockSpec((tm, d), lambda r: (r + off, 0)),
            pl.BlockSpec((1, 1, d), lambda r: mrow(r, 2)),
            pl.BlockSpec((1, 1, d), lambda r: mrow(r, 3)),
            pl.BlockSpec((1, 1, d), lambda r: mrow(r, 4)),
            pl.BlockSpec((1, d), const),
            pl.BlockSpec((d, ROUTER_LANES), const),
            pl.BlockSpec((d, ROUTER_LANES), const),
        ],
        out_specs=[
            pl.BlockSpec((tm, d), lambda r: (r + off, 0)),
            pl.BlockSpec((tm, d // 2), lambda r: (r, 0)),
            pl.BlockSpec((tm, ROUTER_LANES), lambda r: (r, 0)),
        ],
        out_shape=[
            jax.ShapeDtypeStruct((t, d), F32),
            jax.ShapeDtypeStruct((nblk * tm, d // 2), jnp.uint32),
            jax.ShapeDtypeStruct((nblk * tm, ROUTER_LANES), F32),
        ],
        input_output_aliases={2: 0},
        compiler_params=_cparams(1),
        name=f"out_proj_l{layer}",
    )(o, w, xa, modr, modr, modr, gain.reshape(1, d), r_hi, r_lo)


def _inclusive_cumsum_rows(m):
    n, k = m.shape
    blk = LANES
    assert n % blk == 0
    tri = (jnp.arange(blk)[:, None] >= jnp.arange(blk)[None, :]).astype(F32)
    m3 = m.reshape(n // blk, blk, k)
    within = jnp.einsum("ij,bjk->bik", tri, m3, precision=lax.Precision.HIGHEST)
    sums = within[:, -1, :]
    nb = n // blk
    tri_b = (jnp.arange(nb)[:, None] > jnp.arange(nb)[None, :]).astype(F32)
    offs = jnp.dot(tri_b, sums, precision=lax.Precision.HIGHEST)
    return (within + offs[:, None, :]).reshape(n, k)


def _route(logits, tb):
    t = logits.shape[0]
    gl = logits[:, :N_GROUPS]
    g_sel = jnp.argmax(gl, axis=-1)
    g_prob = jax.nn.softmax(gl, axis=-1)
    g_w = jnp.take_along_axis(g_prob, g_sel[:, None], axis=1)[:, 0]
    el = logits[:, N_GROUPS:N_GROUPS + N_EXPERTS].reshape(t, N_GROUPS, EXPERTS_PER_GROUP)
    e_sel = jnp.take_along_axis(el, g_sel[:, None, None], axis=1)[:, 0]
    e_prob = jax.nn.softmax(e_sel, axis=-1)
    lane = jnp.arange(EXPERTS_PER_GROUP, dtype=jnp.int32)[None]
    i1 = jnp.argmax(e_prob, axis=-1).astype(jnp.int32)
    p1 = jnp.max(e_prob, axis=-1)
    rest = jnp.where(lane == i1[:, None], -1.0, e_prob)
    i2 = jnp.argmax(rest, axis=-1).astype(jnp.int32)
    p2 = jnp.max(rest, axis=-1)
    top_p = jnp.stack([p1, p2], axis=-1)
    top_i = jnp.stack([i1, i2], axis=-1)
    top_p = top_p / jnp.sum(top_p, axis=-1, keepdims=True)
    w = g_w[:, None] * top_p
    eid = (g_sel[:, None].astype(jnp.int32) * EXPERTS_PER_GROUP + top_i).reshape(-1)
    a = t * TOP_K
    onehot = (eid[:, None] == jnp.arange(N_EXPERTS, dtype=jnp.int32)[None]).astype(F32)
    csum = _inclusive_cumsum_rows(onehot)
    counts = csum[-1].astype(jnp.int32)
    nblk = (counts + tb - 1) // tb
    bend = jnp.cumsum(nblk)
    bstart = bend - nblk
    slot = jnp.sum(onehot * (csum - 1.0 + (bstart * tb).astype(F32)[None]), axis=-1)
    dest = slot.astype(jnp.int32)
    n_blocks = -(-a // tb) + N_EXPERTS
    blocks = jnp.arange(n_blocks, dtype=jnp.int32)
    block_expert = jnp.minimum(jnp.sum((bend[None, :] <= blocks[:, None]).astype(jnp.int32), axis=1), N_EXPERTS - 1)
    nact = bend[-1:].astype(jnp.int32)
    sorted_tok = (jnp.argsort(eid, stable=True) // TOP_K).astype(jnp.int32)
    start = jnp.cumsum(counts) - counts
    slot_e = jnp.repeat(block_expert, tb)
    off = jnp.arange(n_blocks * tb, dtype=jnp.int32) - bstart[slot_e] * tb
    src = jnp.clip(start[slot_e] + off, 0, a - 1)
    slot_tok = jnp.where(off < counts[slot_e], sorted_tok[src], 0)
    return w, dest, slot_tok, block_expert, nact, n_blocks


def _moe_kernel(be_ref, nact_ref, st_ref, h_hbm, wg_ref, wu_ref, wd_ref, y_ref, xbuf, wg_s, wu_s, wd_s, sem):
    s = pl.program_id(0)
    nact = nact_ref[0]
    tb = xbuf.shape[1]
    blk = s - 1
    do_gather = s < nact
    do_ffn = jnp.logical_and(blk >= 0, blk < nact)
    prev_blk = jnp.maximum(blk - 1, 0)
    fresh = jnp.logical_or(blk == 0, be_ref[jnp.maximum(blk, 0)] != be_ref[prev_blk])

    def gather_rows():
        slot = s % 2
        for r in range(tb):
            pltpu.make_async_copy(h_hbm.at[pl.ds(st_ref[0, 0, r], 1)], xbuf.at[slot, pl.ds(r, 1)],
                                  sem.at[slot]).start()

    def ffn(also_gather):
        slot = blk % 2
        pltpu.make_async_copy(h_hbm.at[pl.ds(0, tb)], xbuf.at[slot], sem.at[slot]).wait()
        xb = _unpack_bf16_pairs(xbuf[slot])
        if also_gather:
            gather_rows()
        g = _dot(xb, wg_s[...])
        u = _dot(xb, wu_s[...])
        act = (g / (1.0 + jnp.exp(-g))) * u
        y_ref[...] = _dot(act.astype(BF16), wd_s[...])

    @pl.when(jnp.logical_and(do_ffn, fresh))
    def _():
        wg_s[...] = wg_ref[...].astype(BF16)
        wu_s[...] = wu_ref[...].astype(BF16)
        wd_s[...] = wd_ref[...].astype(BF16)

    pl.when(jnp.logical_and(do_ffn, do_gather))(lambda: ffn(True))
    pl.when(jnp.logical_and(do_ffn, jnp.logical_not(do_gather)))(lambda: ffn(False))
    pl.when(jnp.logical_and(jnp.logical_not(do_ffn), do_gather))(gather_rows)

    @pl.when(jnp.logical_and(blk >= 0, blk >= nact))
    def _():
        y_ref[...] = jnp.zeros_like(y_ref)


def _moe_call(hp, slot_tok, block_expert, nact, wg, wu, wd, *, n_blocks, tb, layer):
    dw = hp.shape[1]
    d, de = wg.shape[2], wg.shape[3]
    prev = lambda s: jnp.maximum(s - 1, 0)
    grid_spec = pltpu.PrefetchScalarGridSpec(
        num_scalar_prefetch=2,
        grid=(n_blocks + 1,),
        in_specs=[
            pl.BlockSpec((1, 1, tb), lambda s, be, na: (jnp.minimum(s, n_blocks - 1), 0, 0),
                         memory_space=pltpu.SMEM),
            pl.BlockSpec(memory_space=pl.ANY),
            pl.BlockSpec((None, None, d, de), lambda s, be, na: (layer, be[prev(s)], 0, 0)),
            pl.BlockSpec((None, None, d, de), lambda s, be, na: (layer, be[prev(s)], 0, 0)),
            pl.BlockSpec((None, None, de, d), lambda s, be, na: (layer, be[prev(s)], 0, 0)),
        ],
        out_specs=pl.BlockSpec((tb, d), lambda s, be, na: (prev(s), 0)),
        scratch_shapes=[pltpu.VMEM((2, tb, dw), jnp.uint32), pltpu.VMEM((d, de), BF16), pltpu.VMEM((d, de), BF16),
                        pltpu.VMEM((de, d), BF16), pltpu.SemaphoreType.DMA((2,))],
    )
    return pl.pallas_call(
        _moe_kernel,
        grid_spec=grid_spec,
        out_shape=jax.ShapeDtypeStruct((n_blocks * tb, d), F32),
        compiler_params=_cparams(1),
        name=f"moe_experts_l{layer}",
    )(block_expert, nact, slot_tok.reshape(n_blocks, 1, tb), hp, wg, wu, wd)


def _row_gather_start(idx_ref, n_rows, src_hbm, dst, sem, *, idx_stride=1, idx_off=0):
    def body(r, carry):
        tok = idx_ref[0, 0, r * idx_stride + idx_off]
        pltpu.make_async_copy(src_hbm.at[pl.ds(tok, 1)], dst.at[pl.ds(r, 1)], sem).start()
        return carry
    lax.fori_loop(0, n_rows, body, 0, unroll=8)


def _row_gather_wait(n_rows, src_hbm, dst, sem):
    pltpu.make_async_copy(src_hbm.at[pl.ds(0, n_rows)], dst, sem).wait()


def _combine_kernel(pos_ref, y_hbm, w_ref, x_ref, g2_ref, gout_ref, o_ref, ybuf, sem, *, final_norm):
    s = pl.program_id(0)
    nb = pl.num_programs(0) - 1
    tc = ybuf.shape[2]

    @pl.when(s < nb)
    def _():
        slot = s % 2
        for k in range(TOP_K):
            _row_gather_start(pos_ref, tc, y_hbm, ybuf.at[slot, k], sem.at[slot], idx_stride=TOP_K, idx_off=k)

    @pl.when(s >= 1)
    def _():
        slot = (s - 1) % 2
        for k in range(TOP_K):
            _row_gather_wait(tc, y_hbm, ybuf.at[slot, k], sem.at[slot])
        w = w_ref[...]
        moe = ybuf[slot, 0] * w[:, 0:1] + ybuf[slot, 1] * w[:, 1:2]
        xn = x_ref[...] + g2_ref[0] * moe
        if final_norm:
            ms = jnp.mean(xn * xn, axis=-1, keepdims=True)
            xn = xn * lax.rsqrt(ms + RMS_EPS) * gout_ref[...]
        o_ref[...] = xn


def _combine_call(y, dest, w, xa, modr, gout, *, layer, n_ctx_rows, seq, latent_only, final_norm):
    t, d = xa.shape
    tc = 256
    assert n_ctx_rows % tc == 0 and seq % tc == 0
    ncb = n_ctx_rows // tc
    bpb = seq // tc
    off = ncb if latent_only else 0
    nblk = t // tc - off
    base = layer * COND_ROWS
    prev = lambda s: jnp.maximum(s - 1, 0)

    def mrow(s):
        blk = prev(s) + off
        seg = jnp.where(blk < ncb, COND_ROWS // 2, (blk - ncb) // bpb)
        return ((base + seg) * N_MOD + 5, 0, 0)

    out_off = 0 if final_norm else off
    out_rows = nblk * tc if final_norm else t
    return pl.pallas_call(
        functools.partial(_combine_kernel, final_norm=final_norm),
        grid=(nblk + 1,),
        in_specs=[
            pl.BlockSpec((1, 1, TOP_K * tc), lambda s: (jnp.minimum(s, nblk - 1), 0, 0), memory_space=pltpu.SMEM),
            pl.BlockSpec(memory_space=pl.ANY),
            pl.BlockSpec((tc, TOP_K), lambda s: (prev(s), 0)),
            pl.BlockSpec((tc, d), lambda s: (prev(s) + off, 0)),
            pl.BlockSpec((1, 1, d), mrow),
            pl.BlockSpec((1, d), lambda s: (0, 0)),
        ],
        out_specs=pl.BlockSpec((tc, d), lambda s: (prev(s) + out_off, 0)),
        out_shape=jax.ShapeDtypeStruct((out_rows, d), F32),
        scratch_shapes=[pltpu.VMEM((2, TOP_K, tc, d), F32), pltpu.SemaphoreType.DMA((2,))],
        input_output_aliases={} if final_norm else {3: 0},
        compiler_params=_cparams(1),
        name=f"moe_combine_l{layer}",
    )(dest.reshape(nblk, 1, TOP_K * tc), y, w, xa, modr, gout.reshape(1, d))


def kernel(x, c, ctx, c_ctx, ada_w, ada_b, norm_mix_g, norm_ffn_g, norm_out_g, da_w_in, da_w_out,
           da_lam_q1, da_lam_k1, da_lam_q2, da_lam_k2, da_subln_g, gw_w_in, gw_w_out, gw_sinks,
           moe_w_group, moe_w_expert, moe_w_gate, moe_w_up, moe_w_down):
    batch, seq, d = x.shape
    n_ctx = ctx.shape[1]
    tc_rows = batch * n_ctx
    depth = ada_w.shape[0]
    assert batch < COND_ROWS // 2 + 1 and depth == DEPTH

    xa = jnp.concatenate([ctx.reshape(tc_rows, d), x.reshape(batch * seq, d)], axis=0)
    cond = jnp.zeros((COND_ROWS, d), F32).at[:batch].set(c).at[COND_ROWS // 2].set(c_ctx)
    mod = _ada_call(cond, ada_w, ada_b)
    modr = mod.reshape(depth * COND_ROWS * N_MOD, 1, d)

    da_dim = d // (2 * DA_HEADS)
    gw_dim = d // GW_HEADS
    tables_a = _rope_tables(seq, da_dim)
    tables_b = _rope_tables(seq, gw_dim)
    tb = 256
    out = None
    for i in range(depth):
        last = i == depth - 1
        need_ctx = not last
        jm = i // N_MIXERS
        if i % N_MIXERS == 0:
            qkv = _qkv_call(xa, modr, norm_mix_g[i], da_w_in[jm].astype(BF16), tables_a, layer=i,
                            n_ctx_rows=tc_rows, seq=seq, n_rope_cols=2 * d, n_q_cols=d,
                            qscale=da_dim ** -0.5 * LOG2E, rope_dim=da_dim)
            o = _da_attention(qkv, (da_lam_q1[jm], da_lam_k1[jm], da_lam_q2[jm], da_lam_k2[jm]),
                              da_subln_g[jm], layer=i, batch=batch, seq=seq, n_ctx=n_ctx, need_ctx=need_ctx)
            w_out = da_w_out[jm]
        else:
            kv_w = GW_KV_HEADS * gw_dim
            qkv = _qkv_call(xa, modr, norm_mix_g[i], gw_w_in[jm].astype(BF16), tables_b, layer=i,
                            n_ctx_rows=tc_rows, seq=seq, n_rope_cols=d + kv_w, n_q_cols=d,
                            qscale=gw_dim ** -0.5 * LOG2E, rope_dim=gw_dim)
            o = _gw_attention(qkv, gw_sinks[jm], layer=i, batch=batch, seq=seq, n_ctx=n_ctx,
                              need_ctx=need_ctx, d=d)
            w_out = gw_w_out[jm]
        w_r = jnp.concatenate([moe_w_group[i], moe_w_expert[i]], axis=1)
        w_r = jnp.pad(w_r, ((0, 0), (0, ROUTER_LANES - w_r.shape[1])))
        r_hi = w_r.astype(BF16)
        r_lo = (w_r - r_hi.astype(F32)).astype(BF16)
        xa, h2, logits = _oproj_call(o, w_out.astype(BF16), xa, modr, norm_ffn_g[i], r_hi, r_lo, layer=i,
                                     n_ctx_rows=tc_rows, seq=seq, latent_only=last)
        w_tok, dest, slot_tok, block_expert, nact, n_blocks = _route(logits, tb)
        y = _moe_call(h2, slot_tok, block_expert, nact, moe_w_gate, moe_w_up, moe_w_down,
                      n_blocks=n_blocks, tb=tb, layer=i)
        res = _combine_call(y, dest, w_tok, xa, modr, norm_out_g, layer=i, n_ctx_rows=tc_rows, seq=seq,
                            latent_only=last, final_norm=last)
        if last:
            out = res
        else:
            xa = res
    return out.reshape(batch, seq, d)
```

```python
import functools
import math

import jax
import jax.numpy as jnp
from jax import lax
from jax.experimental import pallas as pl
from jax.experimental.pallas import tpu as pltpu

F32 = jnp.float32
BF16 = jnp.bfloat16

DEPTH = 4
N_MIXERS = 2
GRID_W = 64
DA_HEADS = 8
GW_HEADS = 32
GW_KV_HEADS = 4
GW_GROUP = GW_HEADS // GW_KV_HEADS
WINDOW = 128
N_GROUPS = 4
EXPERTS_PER_GROUP = 8
N_EXPERTS = N_GROUPS * EXPERTS_PER_GROUP
TOP_K = 2
ROPE_BASE = 10000.0
RMS_EPS = 1e-6
NEG_INF = -1e30
LOG2E = math.log2(math.e)

LANES = 128
COND_ROWS = 32
N_MOD = 6
ROUTER_LANES = 128
EPILOGUE_ROWS = 256
VMEM_LIMIT = 56 * 1024 * 1024


def _cparams(n_axes, vmem=VMEM_LIMIT):
    return pltpu.CompilerParams(dimension_semantics=("arbitrary",) * n_axes, vmem_limit_bytes=vmem)


def _dot(a, b):
    return jnp.dot(a, b, preferred_element_type=F32)


def _dot_nt(a, b):
    return lax.dot_general(a, b, (((1,), (1,)), ((), ())), preferred_element_type=F32)


def _ada_kernel(c_ref, w_ref, b_ref, o_ref):
    c = c_ref[...]
    s = (c / (1.0 + jnp.exp(-c))).astype(BF16)
    o_ref[...] = _dot(s, w_ref[...].astype(BF16)) + b_ref[...]


def _ada_call(cond, ada_w, ada_b):
    depth, d, n = ada_w.shape
    tn = 1024
    return pl.pallas_call(
        _ada_kernel,
        grid=(depth, n // tn),
        in_specs=[
            pl.BlockSpec((COND_ROWS, d), lambda l, j: (0, 0)),
            pl.BlockSpec((None, d, tn), lambda l, j: (l, 0, j)),
            pl.BlockSpec((None, 1, tn), lambda l, j: (l, 0, j)),
        ],
        out_specs=pl.BlockSpec((None, COND_ROWS, tn), lambda l, j: (l, 0, j)),
        out_shape=jax.ShapeDtypeStruct((depth, COND_ROWS, n), F32),
        compiler_params=_cparams(2),
        name="ada_mod",
    )(cond, ada_w, ada_b.reshape(depth, 1, n))


def _rope_tables(seq, dim):
    rows = seq // GRID_W
    row = jnp.repeat(jnp.arange(rows, dtype=F32), GRID_W)
    col = jnp.tile(jnp.arange(GRID_W, dtype=F32), rows)
    half = dim // 2
    inv = ROPE_BASE ** (-jnp.arange(0, half, 2, dtype=F32) / half)
    ang_r = row[:, None] * inv[None]
    ang_c = col[:, None] * inv[None]
    cr, sr, cc, sc = jnp.cos(ang_r), jnp.sin(ang_r), jnp.cos(ang_c), jnp.sin(ang_c)
    z = jnp.zeros_like(sr)
    reps = LANES // dim
    cos_t = jnp.tile(jnp.concatenate([cr, cr, cc, cc], axis=1), (1, reps))
    sin_a = jnp.tile(jnp.concatenate([-sr, z, -sc, z], axis=1), (1, reps))
    sin_b = jnp.tile(jnp.concatenate([z, sr, z, sc], axis=1), (1, reps))
    return cos_t, sin_a, sin_b


def _epilogue_tables(seq, dim, qscale, tm):
    cos_t, sin_a, sin_b = _rope_tables(seq, dim)
    one = jnp.ones((tm, LANES), F32)
    zero = jnp.zeros((tm, LANES), F32)
    cos_all = jnp.concatenate([cos_t * qscale, cos_t, one, one * qscale], axis=0)
    sa_all = jnp.concatenate([sin_a * qscale, sin_a, zero, zero], axis=0)
    sb_all = jnp.concatenate([sin_b * qscale, sin_b, zero, zero], axis=0)
    return cos_all, sa_all, sb_all


def _qkv_kernel(x_ref, sh_ref, sc_ref, g_ref, w_ref, cos_l, sa_l, sb_l, cos_r, sa_r, sb_r, o_ref, h_scr, *,
                rope_q, tn):
    @pl.when(pl.program_id(1) == 0)
    def _():
        x = x_ref[...]
        ms = jnp.mean(x * x, axis=-1, keepdims=True)
        y = x * lax.rsqrt(ms + RMS_EPS) * g_ref[...]
        h_scr[...] = (y * (1.0 + sc_ref[0]) + sh_ref[0]).astype(BF16)

    n_tiles = tn // LANES
    tm = h_scr.shape[0]
    rows = min(tm, EPILOGUE_ROWS)
    for rc in range(tm // rows):
        rs = slice(rc * rows, (rc + 1) * rows)
        acc = _dot(h_scr[rs, :], w_ref[...])
        for ci in range(n_tiles):
            cos_ref, sa_ref, sb_ref = (cos_l, sa_l, sb_l) if ci < n_tiles // 2 else (cos_r, sa_r, sb_r)
            a = acc[:, ci * LANES:(ci + 1) * LANES]
            a = (a * cos_ref[rs, :] + pltpu.roll(a, LANES - rope_q, 1) * sa_ref[rs, :]
                 + pltpu.roll(a, rope_q, 1) * sb_ref[rs, :])
            o_ref[rs, ci * LANES:(ci + 1) * LANES] = a.astype(BF16)


def _qkv_call(xa, modr, gain, w, *, layer, n_ctx_rows, seq, n_rope_cols, n_q_cols, qscale, rope_dim):
    t, d = xa.shape
    n = w.shape[1]
    tn = 512
    hn = tn // 2
    tm = min(1024, seq, n_ctx_rows)
    assert t % tm == 0 and n_ctx_rows % tm == 0 and seq % tm == 0 and n % tn == 0
    assert n_q_cols % hn == 0 and n_rope_cols % hn == 0
    ncb = n_ctx_rows // tm
    bpb = seq // tm
    base = layer * COND_ROWS
    tables = _epilogue_tables(seq, rope_dim, qscale, tm)

    def seg(r):
        return jnp.where(r < ncb, COND_ROWS // 2, (r - ncb) // bpb)

    def tab_spec(half):
        def idx(r, j):
            col0 = j * tn + half * hn
            p = (r - ncb) % bpb
            lat = jnp.where(col0 < n_q_cols, p, jnp.where(col0 < n_rope_cols, bpb + p, 2 * bpb))
            ctx = jnp.where(col0 < n_q_cols, 2 * bpb + 1, 2 * bpb)
            return (jnp.where(r >= ncb, lat, ctx), 0)
        return pl.BlockSpec((tm, LANES), idx)

    return pl.pallas_call(
        functools.partial(_qkv_kernel, rope_q=rope_dim // 4, tn=tn),
        grid=(t // tm, n // tn),
        in_specs=[
            pl.BlockSpec((tm, d), lambda r, j: (r, 0)),
            pl.BlockSpec((1, 1, d), lambda r, j: ((base + seg(r)) * N_MOD + 0, 0, 0)),
            pl.BlockSpec((1, 1, d), lambda r, j: ((base + seg(r)) * N_MOD + 1, 0, 0)),
            pl.BlockSpec((1, d), lambda r, j: (0, 0)),
            pl.BlockSpec((d, tn), lambda r, j: (0, j)),
        ] + [tab_spec(0)] * 3 + [tab_spec(1)] * 3,
        out_specs=pl.BlockSpec((tm, tn), lambda r, j: (r, j)),
        out_shape=jax.ShapeDtypeStruct((t, n), BF16),
        scratch_shapes=[pltpu.VMEM((tm, d), BF16)],
        compiler_params=_cparams(2),
        name=f"qkv_proj_l{layer}",
    )(xa, modr, modr, gain.reshape(1, d), w, *tables, *tables)


def _da_body(q_ref, kc_ref, vc_ref, kx_ref, vx_ref, lq1, lk1, lq2, lk2, g_ref, o_ref, *, lam_init, has_x):
    dh = q_ref.shape[1] // 2
    lam = (jnp.exp(jnp.sum(lq1[...] * lk1[...], axis=-1, keepdims=True))
           - jnp.exp(jnp.sum(lq2[...] * lk2[...], axis=-1, keepdims=True)) + lam_init)

    def probs(lo):
        qc = q_ref[:, lo:lo + dh]
        s_c = _dot_nt(qc, kc_ref[:, lo:lo + dh])
        m = jnp.max(s_c, axis=-1, keepdims=True)
        if has_x:
            s_x = _dot_nt(qc, kx_ref[:, lo:lo + dh])
            m = jnp.maximum(m, jnp.max(s_x, axis=-1, keepdims=True))
        e_c = jnp.exp2(s_c - m)
        den = jnp.sum(e_c, axis=-1, keepdims=True)
        e_x = None
        if has_x:
            e_x = jnp.exp2(s_x - m)
            den = den + jnp.sum(e_x, axis=-1, keepdims=True)
            e_x = e_x.astype(BF16)
        return e_c.astype(BF16), e_x, 1.0 / den

    e1c, e1x, r1 = probs(0)
    e2c, e2x, r2 = probs(dh)
    tq = q_ref.shape[0]
    ov = _dot(jnp.concatenate([e1c, e2c], axis=0), vc_ref[...])
    if has_x:
        ov = ov + _dot(jnp.concatenate([e1x, e2x], axis=0), vx_ref[...])
    o = ov[:tq] * r1 - ov[tq:] * (lam * r2)
    ms = jnp.mean(o * o, axis=-1, keepdims=True)
    o = o * lax.rsqrt(ms + RMS_EPS) * g_ref[...] * (1.0 - lam_init)
    o_ref[...] = o.astype(BF16)


def _da_kernel(*refs, lam_init, nqb, has_ctx_steps):
    body = functools.partial(_da_body, *refs, lam_init=lam_init)
    if has_ctx_steps:
        qb = pl.program_id(2)
        pl.when(qb < nqb)(lambda: body(has_x=True))
        pl.when(qb >= nqb)(lambda: body(has_x=False))
    else:
        body(has_x=True)


def _da_attention(qkv, lam_vecs, subln_g, *, layer, batch, seq, n_ctx, need_ctx):
    t, n3 = qkv.shape
    d = n3 // 3
    hw = d // DA_HEADS
    tc_rows = batch * n_ctx
    tq = min(256, seq)
    assert seq % tq == 0 and n_ctx % tq == 0 and tc_rows % seq == 0 and hw % LANES == 0
    lam_init = 0.8 - 0.6 * math.exp(-0.3 * layer)
    nh = DA_HEADS
    small = [v.reshape(1, -1) for v in lam_vecs] + [subln_g.reshape(1, hw)]
    small_specs = [pl.BlockSpec(v.shape, lambda b, h, qb: (0, 0)) for v in small]
    qoff = tc_rows // tq
    nqb = seq // tq
    ncq = n_ctx // tq if need_ctx else 0
    xoff = tc_rows // seq

    def q_row(b, qb):
        lat = qoff + b * nqb + jnp.minimum(qb, nqb - 1)
        return jnp.where(qb < nqb, lat, b * ncq + (qb - nqb)) if need_ctx else lat

    def o_row(b, qb):
        return q_row(b, qb) if need_ctx else b * nqb + qb

    return pl.pallas_call(
        functools.partial(_da_kernel, lam_init=lam_init, nqb=nqb, has_ctx_steps=need_ctx),
        grid=(batch, nh, nqb + ncq),
        in_specs=[
            pl.BlockSpec((tq, hw), lambda b, h, qb: (q_row(b, qb), h)),
            pl.BlockSpec((n_ctx, hw), lambda b, h, qb: (b, nh + h)),
            pl.BlockSpec((n_ctx, hw), lambda b, h, qb: (b, 2 * nh + h)),
            pl.BlockSpec((seq, hw), lambda b, h, qb: (xoff + b, nh + h)),
            pl.BlockSpec((seq, hw), lambda b, h, qb: (xoff + b, 2 * nh + h)),
        ] + small_specs,
        out_specs=pl.BlockSpec((tq, hw), lambda b, h, qb: (o_row(b, qb), h)),
        out_shape=jax.ShapeDtypeStruct((t if need_ctx else batch * seq, d), BF16),
        compiler_params=_cparams(3),
        name=f"diff_attn_l{layer}",
    )(qkv, qkv, qkv, qkv, qkv, *small)


def _gw_body(sink_ref, q_ref, kc_ref, vc_ref, kx_ref, vx_ref, o_ref, *, has_x, heads_per_step, group, dim, nqb):
    gp = pl.program_id(1)
    tq = q_ref.shape[0]
    half = LANES // 2
    assert dim == half
    tiles = group * dim // LANES
    kv_heads = heads_per_step // group
    if has_x:
        qb = pl.program_id(2)
        st_p = pl.multiple_of(jnp.maximum(qb - 1, 0) * tq, tq)
        st_m = pl.multiple_of(qb * tq, tq)
        st_n = pl.multiple_of(jnp.minimum(qb + 1, nqb - 1) * tq, tq)
        k_all = jnp.concatenate([kc_ref[...]] + [kx_ref[pl.ds(s, tq), :] for s in (st_p, st_m, st_n)], axis=0)
        v_all = jnp.concatenate([vc_ref[...]] + [vx_ref[pl.ds(s, tq), :] for s in (st_p, st_m, st_n)], axis=0)
        n_ctx = kc_ref.shape[0]
        r = lax.broadcasted_iota(jnp.int32, (tq, tq), 0)
        c = lax.broadcasted_iota(jnp.int32, (tq, tq), 1)
        zero = jnp.zeros((tq, tq), F32)
        bias_p = jnp.where(jnp.logical_and(c >= r, qb >= 1), 0.0, NEG_INF)
        bias_n = jnp.where(jnp.logical_and(c <= r, qb + 1 < nqb), 0.0, NEG_INF)
        bias = jnp.concatenate([jnp.zeros((tq, n_ctx), F32), bias_p, zero, bias_n], axis=1)
        bias = jnp.concatenate([bias] * tiles, axis=0)
    else:
        k_all = kc_ref[...]
        v_all = vc_ref[...]
        bias = None
    lo = lax.broadcasted_iota(jnp.int32, k_all.shape, 1) < half

    def half_variants(x):
        xf = x.astype(F32)
        xs = pltpu.roll(xf, half, 1)
        keep_lo = lambda a: jnp.where(lo, a, 0.0).astype(BF16)
        keep_hi = lambda a: jnp.where(lo, 0.0, a).astype(BF16)
        return [(keep_lo(xf), keep_hi(xs)), (keep_lo(xs), keep_hi(xf))]

    k_var = half_variants(k_all)
    v_var = half_variants(v_all)
    for gi in range(kv_heads):
        kz = k_var[gi]
        vz = v_var[gi]
        q_stack = jnp.concatenate(
            [q_ref[:, (gi * tiles + j) * LANES:(gi * tiles + j + 1) * LANES] for j in range(tiles)], axis=0)
        out = None
        for par in range(2):
            s = _dot_nt(q_stack, kz[par])
            if bias is not None:
                s = s + bias
            head0 = gp * heads_per_step + gi * group + par
            sk = jnp.concatenate(
                [jnp.full((tq, 1), sink_ref[head0 + 2 * j] * LOG2E, F32) for j in range(tiles)], axis=0)
            m = jnp.maximum(jnp.max(s, axis=-1, keepdims=True), sk)
            e = jnp.exp2(s - m)
            den = jnp.sum(e, axis=-1, keepdims=True) + jnp.exp2(sk - m)
            o = _dot(e.astype(BF16), vz[par]) * (1.0 / den)
            out = o if out is None else out + o
        for j in range(tiles):
            o_ref[:, (gi * tiles + j) * LANES:(gi * tiles + j + 1) * LANES] = out[j * tq:(j + 1) * tq].astype(BF16)


def _gw_kernel(*refs, nqb, has_ctx_steps, **kw):
    body = functools.partial(_gw_body, *refs, nqb=nqb, **kw)
    if has_ctx_steps:
        qb = pl.program_id(2)
        pl.when(qb < nqb)(lambda: body(has_x=True))
        pl.when(qb >= nqb)(lambda: body(has_x=False))
    else:
        body(has_x=True)


def _gw_attention(qkv, sinks, *, layer, batch, seq, n_ctx, need_ctx, d):
    t = qkv.shape[0]
    dim = d // GW_HEADS
    kv_per_step = LANES // dim
    hps = kv_per_step * GW_GROUP
    qw = hps * dim
    n_gp = GW_KV_HEADS // kv_per_step
    tq = WINDOW
    tc_rows = batch * n_ctx
    assert seq % tq == 0 and n_ctx % tq == 0 and tc_rows % seq == 0
    nqb = seq // tq
    ncq = n_ctx // tq if need_ctx else 0
    qoff = tc_rows // tq
    xoff = tc_rows // seq
    kcol = d // LANES
    vcol = kcol + GW_KV_HEADS * dim // LANES

    def q_row(b, qb):
        lat = qoff + b * nqb + jnp.minimum(qb, nqb - 1)
        return jnp.where(qb < nqb, lat, b * ncq + (qb - nqb)) if need_ctx else lat

    def o_row(b, qb):
        return q_row(b, qb) if need_ctx else b * nqb + qb

    return pl.pallas_call(
        functools.partial(_gw_kernel, nqb=nqb, has_ctx_steps=need_ctx, heads_per_step=hps, group=GW_GROUP, dim=dim),
        grid=(batch, n_gp, nqb + ncq),
        in_specs=[
            pl.BlockSpec(memory_space=pltpu.SMEM),
            pl.BlockSpec((tq, qw), lambda b, g, qb: (q_row(b, qb), g)),
            pl.BlockSpec((n_ctx, LANES), lambda b, g, qb: (b, kcol + g)),
            pl.BlockSpec((n_ctx, LANES), lambda b, g, qb: (b, vcol + g)),
            pl.BlockSpec((seq, LANES), lambda b, g, qb: (xoff + b, kcol + g)),
            pl.BlockSpec((seq, LANES), lambda b, g, qb: (xoff + b, vcol + g)),
        ],
        out_specs=pl.BlockSpec((tq, qw), lambda b, g, qb: (o_row(b, qb), g)),
        out_shape=jax.ShapeDtypeStruct((t if need_ctx else batch * seq, d), BF16),
        compiler_params=_cparams(3),
        name=f"gqa_attn_l{layer}",
    )(sinks, qkv, qkv, qkv, qkv, qkv)


def _pack_bf16_pairs(h):
    half = h.shape[1] // 2
    hi = lax.bitcast_convert_type(h[:, :half].astype(BF16).astype(F32), jnp.uint32)
    lo = lax.bitcast_convert_type(h[:, half:].astype(BF16).astype(F32), jnp.uint32)
    return hi | (lo >> 16)


def _unpack_bf16_pairs(u):
    hi = lax.bitcast_convert_type(u & jnp.uint32(0xFFFF0000), F32)
    lo = lax.bitcast_convert_type(u << 16, F32)
    return jnp.concatenate([hi, lo], axis=1).astype(BF16)


def _oproj_kernel(o_ref, w_ref, x_ref, g1_ref, sh_ref, sc_ref, gn_ref, rhi_ref, rlo_ref,
                  xo_ref, h_ref, lg_ref):
    xn = x_ref[...] + g1_ref[0] * _dot(o_ref[...], w_ref[...])
    xo_ref[...] = xn
    ms = jnp.mean(xn * xn, axis=-1, keepdims=True)
    h = xn * lax.rsqrt(ms + RMS_EPS) * gn_ref[...]
    h = h * (1.0 + sc_ref[0]) + sh_ref[0]
    h_ref[...] = _pack_bf16_pairs(h)
    h_hi = h.astype(BF16)
    h_lo = (h - h_hi.astype(F32)).astype(BF16)
    lg_ref[...] = _dot(h_hi, rhi_ref[...]) + (_dot(h_hi, rlo_ref[...]) + _dot(h_lo, rhi_ref[...]))


def _oproj_call(o, w, xa, modr, gain, r_hi, r_lo, *, layer, n_ctx_rows, seq, latent_only):
    t, d = xa.shape
    tm = 256
    assert n_ctx_rows % tm == 0 and seq % tm == 0
    ncb = n_ctx_rows // tm
    bpb = seq // tm
    off = ncb if latent_only else 0
    nblk = t // tm - off
    base = layer * COND_ROWS

    def mrow(r, chunk):
        blk = r + off
        seg = jnp.where(blk < ncb, COND_ROWS // 2, (blk - ncb) // bpb)
        return ((base + seg) * N_MOD + chunk, 0, 0)

    const = lambda r: (0, 0)
    return pl.pallas_call(
        _oproj_kernel,
        grid=(nblk,),
        in_specs=[
            pl.BlockSpec((tm, d), lambda r: (r, 0)),
            pl.BlockSpec((d, d), const, pipeline_mode=pl.Buffered(1)),
            pl.BlockSpec((tm, d), lambda r: (r + off, 0)),
            pl.BlockSpec((1, 1, d), lambda r: mrow(r, 2)),
            pl.BlockSpec((1, 1, d), lambda r: mrow(r, 3)),
            pl.BlockSpec((1, 1, d), lambda r: mrow(r, 4)),
            pl.BlockSpec((1, d), const),
            pl.BlockSpec((d, ROUTER_LANES), const),
            pl.BlockSpec((d, ROUTER_LANES), const),
        ],
        out_specs=[
            pl.BlockSpec((tm, d), lambda r: (r + off, 0)),
            pl.BlockSpec((tm, d // 2), lambda r: (r, 0)),
            pl.BlockSpec((tm, ROUTER_LANES), lambda r: (r, 0)),
        ],
        out_shape=[
            jax.ShapeDtypeStruct((t, d), F32),
            jax.ShapeDtypeStruct((nblk * tm, d // 2), jnp.uint32),
            jax.ShapeDtypeStruct((nblk * tm, ROUTER_LANES), F32),
        ],
        input_output_aliases={2: 0},
        compiler_params=_cparams(1),
        name=f"out_proj_l{layer}",
    )(o, w, xa, modr, modr, modr, gain.reshape(1, d), r_hi, r_lo)


def _inclusive_cumsum_rows(m):
    n, k = m.shape
    blk = LANES
    assert n % blk == 0
    tri = (jnp.arange(blk)[:, None] >= jnp.arange(blk)[None, :]).astype(F32)
    m3 = m.reshape(n // blk, blk, k)
    within = jnp.einsum("ij,bjk->bik", tri, m3, precision=lax.Precision.HIGHEST)
    sums = within[:, -1, :]
    nb = n // blk
    tri_b = (jnp.arange(nb)[:, None] > jnp.arange(nb)[None, :]).astype(F32)
    offs = jnp.dot(tri_b, sums, precision=lax.Precision.HIGHEST)
    return (within + offs[:, None, :]).reshape(n, k)


def _route(logits, tb):
    t = logits.shape[0]
    gl = logits[:, :N_GROUPS]
    g_sel = jnp.argmax(gl, axis=-1)
    g_prob = jax.nn.softmax(gl, axis=-1)
    g_w = jnp.take_along_axis(g_prob, g_sel[:, None], axis=1)[:, 0]
    el = logits[:, N_GROUPS:N_GROUPS + N_EXPERTS].reshape(t, N_GROUPS, EXPERTS_PER_GROUP)
    e_sel = jnp.take_along_axis(el, g_sel[:, None, None], axis=1)[:, 0]
    e_prob = jax.nn.softmax(e_sel, axis=-1)
    lane = jnp.arange(EXPERTS_PER_GROUP, dtype=jnp.int32)[None]
    i1 = jnp.argmax(e_prob, axis=-1).astype(jnp.int32)
    p1 = jnp.max(e_prob, axis=-1)
    rest = jnp.where(lane == i1[:, None], -1.0, e_prob)
    i2 = jnp.argmax(rest, axis=-1).astype(jnp.int32)
    p2 = jnp.max(rest, axis=-1)
    top_p = jnp.stack([p1, p2], axis=-1)
    top_i = jnp.stack([i1, i2], axis=-1)
    top_p = top_p / jnp.sum(top_p, axis=-1, keepdims=True)
    w = g_w[:, None] * top_p
    eid = (g_sel[:, None].astype(jnp.int32) * EXPERTS_PER_GROUP + top_i).reshape(-1)
    a = t * TOP_K
    onehot = (eid[:, None] == jnp.arange(N_EXPERTS, dtype=jnp.int32)[None]).astype(F32)
    csum = _inclusive_cumsum_rows(onehot)
    counts = csum[-1].astype(jnp.int32)
    nblk = (counts + tb - 1) // tb
    bend = jnp.cumsum(nblk)
    bstart = bend - nblk
    slot = jnp.sum(onehot * (csum - 1.0 + (bstart * tb).astype(F32)[None]), axis=-1)
    dest = slot.astype(jnp.int32)
    n_blocks = -(-a // tb) + N_EXPERTS
    blocks = jnp.arange(n_blocks, dtype=jnp.int32)
    block_expert = jnp.minimum(jnp.sum((bend[None, :] <= blocks[:, None]).astype(jnp.int32), axis=1), N_EXPERTS - 1)
    nact = bend[-1:].astype(jnp.int32)
    sorted_tok = (jnp.argsort(eid, stable=True) // TOP_K).astype(jnp.int32)
    start = jnp.cumsum(counts) - counts
    slot_e = jnp.repeat(block_expert, tb)
    off = jnp.arange(n_blocks * tb, dtype=jnp.int32) - bstart[slot_e] * tb
    src = jnp.clip(start[slot_e] + off, 0, a - 1)
    slot_tok = jnp.where(off < counts[slot_e], sorted_tok[src], 0)
    return w, dest, slot_tok, block_expert, nact, n_blocks


def _moe_kernel(be_ref, nact_ref, st_ref, h_hbm, wg_ref, wu_ref, wd_ref, y_ref, xbuf, wg_s, wu_s, wd_s, sem):
    s = pl.program_id(0)
    nact = nact_ref[0]
    tb = xbuf.shape[1]
    blk = s - 1
    do_gather = s < nact
    do_ffn = jnp.logical_and(blk >= 0, blk < nact)
    prev_blk = jnp.maximum(blk - 1, 0)
    fresh = jnp.logical_or(blk == 0, be_ref[jnp.maximum(blk, 0)] != be_ref[prev_blk])

    def gather_row(r, slot):
        pltpu.make_async_copy(h_hbm.at[pl.ds(st_ref[0, 0, r], 1)], xbuf.at[slot, pl.ds(r, 1)], sem.at[slot]).start()

    def gather_rows():
        for r in range(tb):
            gather_row(r, s % 2)

    def gather_rows_rolled():
        lax.fori_loop(0, tb, lambda r, c: (gather_row(r, s % 2), c)[1], 0, unroll=8)

    def ffn(also_gather):
        slot = blk % 2
        pltpu.make_async_copy(h_hbm.at[pl.ds(0, tb)], xbuf.at[slot], sem.at[slot]).wait()
        xb = _unpack_bf16_pairs(xbuf[slot])
        if also_gather:
            gather_rows()
        g = _dot(xb, wg_s[...])
        u = _dot(xb, wu_s[...])
        act = (g / (1.0 + jnp.exp(-g))) * u
        y_ref[...] = _dot(act.astype(BF16), wd_s[...])

    @pl.when(jnp.logical_and(do_ffn, fresh))
    def _():
        wg_s[...] = wg_ref[...].astype(BF16)
        wu_s[...] = wu_ref[...].astype(BF16)
        wd_s[...] = wd_ref[...].astype(BF16)

    pl.when(jnp.logical_and(do_ffn, do_gather))(lambda: ffn(True))
    pl.when(jnp.logical_and(do_ffn, jnp.logical_not(do_gather)))(lambda: ffn(False))
    pl.when(jnp.logical_and(jnp.logical_not(do_ffn), do_gather))(gather_rows_rolled)

    @pl.when(jnp.logical_and(blk >= 0, blk >= nact))
    def _():
        y_ref[...] = jnp.zeros_like(y_ref)


def _moe_call(hp, slot_tok, block_expert, nact, wg, wu, wd, *, n_blocks, tb, layer):
    dw = hp.shape[1]
    d, de = wg.shape[2], wg.shape[3]
    prev = lambda s: jnp.maximum(s - 1, 0)
    grid_spec = pltpu.PrefetchScalarGridSpec(
        num_scalar_prefetch=2,
        grid=(n_blocks + 1,),
        in_specs=[
            pl.BlockSpec((1, 1, tb), lambda s, be, na: (jnp.minimum(s, n_blocks - 1), 0, 0),
                         memory_space=pltpu.SMEM),
            pl.BlockSpec(memory_space=pl.ANY),
            pl.BlockSpec((None, None, d, de), lambda s, be, na: (layer, be[prev(s)], 0, 0)),
            pl.BlockSpec((None, None, d, de), lambda s, be, na: (layer, be[prev(s)], 0, 0)),
            pl.BlockSpec((None, None, de, d), lambda s, be, na: (layer, be[prev(s)], 0, 0)),
        ],
        out_specs=pl.BlockSpec((tb, d), lambda s, be, na: (prev(s), 0)),
        scratch_shapes=[pltpu.VMEM((2, tb, dw), jnp.uint32), pltpu.VMEM((d, de), BF16), pltpu.VMEM((d, de), BF16),
                        pltpu.VMEM((de, d), BF16), pltpu.SemaphoreType.DMA((2,))],
    )
    return pl.pallas_call(
        _moe_kernel,
        grid_spec=grid_spec,
        out_shape=jax.ShapeDtypeStruct((n_blocks * tb, d), F32),
        compiler_params=_cparams(1),
        name=f"moe_experts_l{layer}",
    )(block_expert, nact, slot_tok.reshape(n_blocks, 1, tb), hp, wg, wu, wd)


def _row_gather_start(idx_ref, n_rows, src_hbm, dst, sem, *, idx_stride=1, idx_off=0):
    def body(r, carry):
        tok = idx_ref[0, 0, r * idx_stride + idx_off]
        pltpu.make_async_copy(src_hbm.at[pl.ds(tok, 1)], dst.at[pl.ds(r, 1)], sem).start()
        return carry
    lax.fori_loop(0, n_rows, body, 0, unroll=8)


def _row_gather_wait(n_rows, src_hbm, dst, sem):
    pltpu.make_async_copy(src_hbm.at[pl.ds(0, n_rows)], dst, sem).wait()


def _combine_kernel(pos_ref, y_hbm, w_ref, x_ref, g2_ref, gout_ref, o_ref, ybuf, sem, *, final_norm):
    s = pl.program_id(0)
    nb = pl.num_programs(0) - 1
    tc = ybuf.shape[2]

    @pl.when(s < nb)
    def _():
        slot = s % 2
        for k in range(TOP_K):
            _row_gather_start(pos_ref, tc, y_hbm, ybuf.at[slot, k], sem.at[slot], idx_stride=TOP_K, idx_off=k)

    @pl.when(s >= 1)
    def _():
        slot = (s - 1) % 2
        for k in range(TOP_K):
            _row_gather_wait(tc, y_hbm, ybuf.at[slot, k], sem.at[slot])
        w = w_ref[...]
        moe = ybuf[slot, 0] * w[:, 0:1] + ybuf[slot, 1] * w[:, 1:2]
        xn = x_ref[...] + g2_ref[0] * moe
        if final_norm:
            ms = jnp.mean(xn * xn, axis=-1, keepdims=True)
            xn = xn * lax.rsqrt(ms + RMS_EPS) * gout_ref[...]
        o_ref[...] = xn


def _combine_call(y, dest, w, xa, modr, gout, *, layer, n_ctx_rows, seq, latent_only, final_norm):
    t, d = xa.shape
    tc = 256
    assert n_ctx_rows % tc == 0 and seq % tc == 0
    ncb = n_ctx_rows // tc
    bpb = seq // tc
    off = ncb if latent_only else 0
    nblk = t // tc - off
    base = layer * COND_ROWS
    prev = lambda s: jnp.maximum(s - 1, 0)

    def mrow(s):
        blk = prev(s) + off
        seg = jnp.where(blk < ncb, COND_ROWS // 2, (blk - ncb) // bpb)
        return ((base + seg) * N_MOD + 5, 0, 0)

    out_off = 0 if final_norm else off
    out_rows = nblk * tc if final_norm else t
    return pl.pallas_call(
        functools.partial(_combine_kernel, final_norm=final_norm),
        grid=(nblk + 1,),
        in_specs=[
            pl.BlockSpec((1, 1, TOP_K * tc), lambda s: (jnp.minimum(s, nblk - 1), 0, 0), memory_space=pltpu.SMEM),
            pl.BlockSpec(memory_space=pl.ANY),
            pl.BlockSpec((tc, TOP_K), lambda s: (prev(s), 0)),
            pl.BlockSpec((tc, d), lambda s: (prev(s) + off, 0)),
            pl.BlockSpec((1, 1, d), mrow),
            pl.BlockSpec((1, d), lambda s: (0, 0)),
        ],
        out_specs=pl.BlockSpec((tc, d), lambda s: (prev(s) + out_off, 0)),
        out_shape=jax.ShapeDtypeStruct((out_rows, d), F32),
        scratch_shapes=[pltpu.VMEM((2, TOP_K, tc, d), F32), pltpu.SemaphoreType.DMA((2,))],
        input_output_aliases={} if final_norm else {3: 0},
        compiler_params=_cparams(1),
        name=f"moe_combine_l{layer}",
    )(dest.reshape(nblk, 1, TOP_K * tc), y, w, xa, modr, gout.reshape(1, d))


def kernel(x, c, ctx, c_ctx, ada_w, ada_b, norm_mix_g, norm_ffn_g, norm_out_g, da_w_in, da_w_out,
           da_lam_q1, da_lam_k1, da_lam_q2, da_lam_k2, da_subln_g, gw_w_in, gw_w_out, gw_sinks,
           moe_w_group, moe_w_expert, moe_w_gate, moe_w_up, moe_w_down):
    batch, seq, d = x.shape
    n_ctx = ctx.shape[1]
    tc_rows = batch * n_ctx
    depth = ada_w.shape[0]
    assert batch < COND_ROWS // 2 + 1 and depth == DEPTH

    xa = jnp.concatenate([ctx.reshape(tc_rows, d), x.reshape(batch * seq, d)], axis=0)
    cond = jnp.zeros((COND_ROWS, d), F32).at[:batch].set(c).at[COND_ROWS // 2].set(c_ctx)
    mod = _ada_call(cond, ada_w, ada_b)
    modr = mod.reshape(depth * COND_ROWS * N_MOD, 1, d)

    da_dim = d // (2 * DA_HEADS)
    gw_dim = d // GW_HEADS
    tb = 256
    out = None
    for i in range(depth):
        last = i == depth - 1
        need_ctx = not last
        jm = i // N_MIXERS
        if i % N_MIXERS == 0:
            qkv = _qkv_call(xa, modr, norm_mix_g[i], da_w_in[jm].astype(BF16), layer=i,
                            n_ctx_rows=tc_rows, seq=seq, n_rope_cols=2 * d, n_q_cols=d,
                            qscale=da_dim ** -0.5 * LOG2E, rope_dim=da_dim)
            o = _da_attention(qkv, (da_lam_q1[jm], da_lam_k1[jm], da_lam_q2[jm], da_lam_k2[jm]),
                              da_subln_g[jm], layer=i, batch=batch, seq=seq, n_ctx=n_ctx, need_ctx=need_ctx)
            w_out = da_w_out[jm]
        else:
            kv_w = GW_KV_HEADS * gw_dim
            qkv = _qkv_call(xa, modr, norm_mix_g[i], gw_w_in[jm].astype(BF16), layer=i,
                            n_ctx_rows=tc_rows, seq=seq, n_rope_cols=d + kv_w, n_q_cols=d,
                            qscale=gw_dim ** -0.5 * LOG2E, rope_dim=gw_dim)
            o = _gw_attention(qkv, gw_sinks[jm], layer=i, batch=batch, seq=seq, n_ctx=n_ctx,
                              need_ctx=need_ctx, d=d)
            w_out = gw_w_out[jm]
        w_r = jnp.concatenate([moe_w_group[i], moe_w_expert[i]], axis=1)
        w_r = jnp.pad(w_r, ((0, 0), (0, ROUTER_LANES - w_r.shape[1])))
        r_hi = w_r.astype(BF16)
        r_lo = (w_r - r_hi.astype(F32)).astype(BF16)
        xa, h2, logits = _oproj_call(o, w_out.astype(BF16), xa, modr, norm_ffn_g[i], r_hi, r_lo, layer=i,
                                     n_ctx_rows=tc_rows, seq=seq, latent_only=last)
        w_tok, dest, slot_tok, block_expert, nact, n_blocks = _route(logits, tb)
        y = _moe_call(h2, slot_tok, block_expert, nact, moe_w_gate, moe_w_up, moe_w_down,
                      n_blocks=n_blocks, tb=tb, layer=i)
        res = _combine_call(y, dest, w_tok, xa, modr, norm_out_g, layer=i, n_ctx_rows=tc_rows, seq=seq,
                            latent_only=last, final_norm=last)
        if last:
            out = res
        else:
            xa = res
    return out.reshape(batch, seq, d)
```

```python
import functools
import math

import jax
import jax.numpy as jnp
from jax import lax
from jax.experimental import pallas as pl
from jax.experimental.pallas import tpu as pltpu

F32 = jnp.float32
BF16 = jnp.bfloat16

DEPTH = 4
N_MIXERS = 2
GRID_W = 64
DA_HEADS = 8
GW_HEADS = 32
GW_KV_HEADS = 4
GW_GROUP = GW_HEADS // GW_KV_HEADS
WINDOW = 128
N_GROUPS = 4
EXPERTS_PER_GROUP = 8
N_EXPERTS = N_GROUPS * EXPERTS_PER_GROUP
TOP_K = 2
ROPE_BASE = 10000.0
RMS_EPS = 1e-6
NEG_INF = -1e30
LOG2E = math.log2(math.e)

LANES = 128
COND_ROWS = 32
N_MOD = 6
ROUTER_LANES = 128
EPILOGUE_ROWS = 256
VMEM_LIMIT = 56 * 1024 * 1024


def _cparams(n_axes, vmem=VMEM_LIMIT):
    return pltpu.CompilerParams(dimension_semantics=("arbitrary",) * n_axes, vmem_limit_bytes=vmem)


def _dot(a, b):
    return jnp.dot(a, b, preferred_element_type=F32)


def _dot_nt(a, b):
    return lax.dot_general(a, b, (((1,), (1,)), ((), ())), preferred_element_type=F32)


def _ada_kernel(c_ref, w_ref, b_ref, o_ref):
    c = c_ref[...]
    s = (c / (1.0 + jnp.exp(-c))).astype(BF16)
    o_ref[...] = _dot(s, w_ref[...].astype(BF16)) + b_ref[...]


def _ada_call(cond, ada_w, ada_b):
    depth, d, n = ada_w.shape
    tn = 1024
    return pl.pallas_call(
        _ada_kernel,
        grid=(depth, n // tn),
        in_specs=[
            pl.BlockSpec((COND_ROWS, d), lambda l, j: (0, 0)),
            pl.BlockSpec((None, d, tn), lambda l, j: (l, 0, j)),
            pl.BlockSpec((None, 1, tn), lambda l, j: (l, 0, j)),
        ],
        out_specs=pl.BlockSpec((None, COND_ROWS, tn), lambda l, j: (l, 0, j)),
        out_shape=jax.ShapeDtypeStruct((depth, COND_ROWS, n), F32),
        compiler_params=_cparams(2),
        name="ada_mod",
    )(cond, ada_w, ada_b.reshape(depth, 1, n))


def _rope_tables(seq, dim):
    rows = seq // GRID_W
    row = jnp.repeat(jnp.arange(rows, dtype=F32), GRID_W)
    col = jnp.tile(jnp.arange(GRID_W, dtype=F32), rows)
    half = dim // 2
    inv = ROPE_BASE ** (-jnp.arange(0, half, 2, dtype=F32) / half)
    ang_r = row[:, None] * inv[None]
    ang_c = col[:, None] * inv[None]
    cr, sr, cc, sc = jnp.cos(ang_r), jnp.sin(ang_r), jnp.cos(ang_c), jnp.sin(ang_c)
    z = jnp.zeros_like(sr)
    reps = LANES // dim
    cos_t = jnp.tile(jnp.concatenate([cr, cr, cc, cc], axis=1), (1, reps))
    sin_a = jnp.tile(jnp.concatenate([-sr, z, -sc, z], axis=1), (1, reps))
    sin_b = jnp.tile(jnp.concatenate([z, sr, z, sc], axis=1), (1, reps))
    return cos_t, sin_a, sin_b


def _epilogue_tables(seq, dim, qscale, tm):
    cos_t, sin_a, sin_b = _rope_tables(seq, dim)
    one = jnp.ones((tm, LANES), F32)
    zero = jnp.zeros((tm, LANES), F32)
    cos_all = jnp.concatenate([cos_t * qscale, cos_t, one, one * qscale], axis=0)
    sa_all = jnp.concatenate([sin_a * qscale, sin_a, zero, zero], axis=0)
    sb_all = jnp.concatenate([sin_b * qscale, sin_b, zero, zero], axis=0)
    return cos_all, sa_all, sb_all


def _qkv_kernel(x_ref, sh_ref, sc_ref, g_ref, w_ref, cos_l, sa_l, sb_l, cos_r, sa_r, sb_r, o_ref, h_scr, *,
                rope_q, tn):
    @pl.when(pl.program_id(1) == 0)
    def _():
        x = x_ref[...]
        ms = jnp.mean(x * x, axis=-1, keepdims=True)
        y = x * lax.rsqrt(ms + RMS_EPS) * g_ref[...]
        h_scr[...] = (y * (1.0 + sc_ref[0]) + sh_ref[0]).astype(BF16)

    n_tiles = tn // LANES
    tm = h_scr.shape[0]
    rows = min(tm, EPILOGUE_ROWS)
    for rc in range(tm // rows):
        rs = slice(rc * rows, (rc + 1) * rows)
        acc = _dot(h_scr[rs, :], w_ref[...])
        for ci in range(n_tiles):
            cos_ref, sa_ref, sb_ref = (cos_l, sa_l, sb_l) if ci < n_tiles // 2 else (cos_r, sa_r, sb_r)
            a = acc[:, ci * LANES:(ci + 1) * LANES]
            a = (a * cos_ref[rs, :] + pltpu.roll(a, LANES - rope_q, 1) * sa_ref[rs, :]
                 + pltpu.roll(a, rope_q, 1) * sb_ref[rs, :])
            o_ref[rs, ci * LANES:(ci + 1) * LANES] = a.astype(BF16)


def _qkv_call(xa, modr, gain, w, *, layer, n_ctx_rows, seq, n_rope_cols, n_q_cols, qscale, rope_dim):
    t, d = xa.shape
    n = w.shape[1]
    tn = 512
    hn = tn // 2
    tm = min(1024, seq, n_ctx_rows)
    assert t % tm == 0 and n_ctx_rows % tm == 0 and seq % tm == 0 and n % tn == 0
    assert n_q_cols % hn == 0 and n_rope_cols % hn == 0
    ncb = n_ctx_rows // tm
    bpb = seq // tm
    base = layer * COND_ROWS
    tables = _epilogue_tables(seq, rope_dim, qscale, tm)

    def seg(r):
        return jnp.where(r < ncb, COND_ROWS // 2, (r - ncb) // bpb)

    def tab_spec(half):
        def idx(r, j):
            col0 = j * tn + half * hn
            p = (r - ncb) % bpb
            lat = jnp.where(col0 < n_q_cols, p, jnp.where(col0 < n_rope_cols, bpb + p, 2 * bpb))
            ctx = jnp.where(col0 < n_q_cols, 2 * bpb + 1, 2 * bpb)
            return (jnp.where(r >= ncb, lat, ctx), 0)
        return pl.BlockSpec((tm, LANES), idx)

    return pl.pallas_call(
        functools.partial(_qkv_kernel, rope_q=rope_dim // 4, tn=tn),
        grid=(t // tm, n // tn),
        in_specs=[
            pl.BlockSpec((tm, d), lambda r, j: (r, 0)),
            pl.BlockSpec((1, 1, d), lambda r, j: ((base + seg(r)) * N_MOD + 0, 0, 0)),
            pl.BlockSpec((1, 1, d), lambda r, j: ((base + seg(r)) * N_MOD + 1, 0, 0)),
            pl.BlockSpec((1, d), lambda r, j: (0, 0)),
            pl.BlockSpec((d, tn), lambda r, j: (0, j)),
        ] + [tab_spec(0)] * 3 + [tab_spec(1)] * 3,
        out_specs=pl.BlockSpec((tm, tn), lambda r, j: (r, j)),
        out_shape=jax.ShapeDtypeStruct((t, n), BF16),
        scratch_shapes=[pltpu.VMEM((tm, d), BF16)],
        compiler_params=_cparams(2),
        name=f"qkv_proj_l{layer}",
    )(xa, modr, modr, gain.reshape(1, d), w, *tables, *tables)


def _da_body(q_ref, kc_ref, vc_ref, kx_ref, vx_ref, lq1, lk1, lq2, lk2, g_ref, o_ref, *, lam_init, has_x):
    dh = q_ref.shape[1] // 2
    lam = (jnp.exp(jnp.sum(lq1[...] * lk1[...], axis=-1, keepdims=True))
           - jnp.exp(jnp.sum(lq2[...] * lk2[...], axis=-1, keepdims=True)) + lam_init)

    def probs(lo):
        qc = q_ref[:, lo:lo + dh]
        s_c = _dot_nt(qc, kc_ref[:, lo:lo + dh])
        m = jnp.max(s_c, axis=-1, keepdims=True)
        if has_x:
            s_x = _dot_nt(qc, kx_ref[:, lo:lo + dh])
            m = jnp.maximum(m, jnp.max(s_x, axis=-1, keepdims=True))
        e_c = jnp.exp2(s_c - m)
        den = jnp.sum(e_c, axis=-1, keepdims=True)
        e_x = None
        if has_x:
            e_x = jnp.exp2(s_x - m)
            den = den + jnp.sum(e_x, axis=-1, keepdims=True)
            e_x = e_x.astype(BF16)
        return e_c.astype(BF16), e_x, 1.0 / den

    e1c, e1x, r1 = probs(0)
    e2c, e2x, r2 = probs(dh)
    tq = q_ref.shape[0]
    ov = _dot(jnp.concatenate([e1c, e2c], axis=0), vc_ref[...])
    if has_x:
        ov = ov + _dot(jnp.concatenate([e1x, e2x], axis=0), vx_ref[...])
    o = ov[:tq] * r1 - ov[tq:] * (lam * r2)
    ms = jnp.mean(o * o, axis=-1, keepdims=True)
    o = o * lax.rsqrt(ms + RMS_EPS) * g_ref[...] * (1.0 - lam_init)
    o_ref[...] = o.astype(BF16)


def _da_kernel(*refs, lam_init, nqb, has_ctx_steps):
    body = functools.partial(_da_body, *refs, lam_init=lam_init)
    if has_ctx_steps:
        qb = pl.program_id(2)
        pl.when(qb < nqb)(lambda: body(has_x=True))
        pl.when(qb >= nqb)(lambda: body(has_x=False))
    else:
        body(has_x=True)


def _da_attention(qkv, lam_vecs, subln_g, *, layer, batch, seq, n_ctx, need_ctx):
    t, n3 = qkv.shape
    d = n3 // 3
    hw = d // DA_HEADS
    tc_rows = batch * n_ctx
    tq = min(256, seq)
    assert seq % tq == 0 and n_ctx % tq == 0 and tc_rows % seq == 0 and hw % LANES == 0
    lam_init = 0.8 - 0.6 * math.exp(-0.3 * layer)
    nh = DA_HEADS
    small = [v.reshape(1, -1) for v in lam_vecs] + [subln_g.reshape(1, hw)]
    small_specs = [pl.BlockSpec(v.shape, lambda b, h, qb: (0, 0)) for v in small]
    qoff = tc_rows // tq
    nqb = seq // tq
    ncq = n_ctx // tq if need_ctx else 0
    xoff = tc_rows // seq

    def q_row(b, qb):
        lat = qoff + b * nqb + jnp.minimum(qb, nqb - 1)
        return jnp.where(qb < nqb, lat, b * ncq + (qb - nqb)) if need_ctx else lat

    def o_row(b, qb):
        return q_row(b, qb) if need_ctx else b * nqb + qb

    return pl.pallas_call(
        functools.partial(_da_kernel, lam_init=lam_init, nqb=nqb, has_ctx_steps=need_ctx),
        grid=(batch, nh, nqb + ncq),
        in_specs=[
            pl.BlockSpec((tq, hw), lambda b, h, qb: (q_row(b, qb), h)),
            pl.BlockSpec((n_ctx, hw), lambda b, h, qb: (b, nh + h)),
            pl.BlockSpec((n_ctx, hw), lambda b, h, qb: (b, 2 * nh + h)),
            pl.BlockSpec((seq, hw), lambda b, h, qb: (xoff + b, nh + h)),
            pl.BlockSpec((seq, hw), lambda b, h, qb: (xoff + b, 2 * nh + h)),
        ] + small_specs,
        out_specs=pl.BlockSpec((tq, hw), lambda b, h, qb: (o_row(b, qb), h)),
        out_shape=jax.ShapeDtypeStruct((t if need_ctx else batch * seq, d), BF16),
        compiler_params=_cparams(3),
        name=f"diff_attn_l{layer}",
    )(qkv, qkv, qkv, qkv, qkv, *small)


def _gw_body(sink_ref, q_ref, kc_ref, vc_ref, kx_ref, vx_ref, o_ref, *, has_x, heads_per_step, group, dim, nqb):
    gp = pl.program_id(1)
    tq = q_ref.shape[0]
    half = LANES // 2
    assert dim == half
    tiles = group * dim // LANES
    kv_heads = heads_per_step // group
    if has_x:
        qb = pl.program_id(2)
        st_p = pl.multiple_of(jnp.maximum(qb - 1, 0) * tq, tq)
        st_m = pl.multiple_of(qb * tq, tq)
        st_n = pl.multiple_of(jnp.minimum(qb + 1, nqb - 1) * tq, tq)
        k_all = jnp.concatenate([kc_ref[...]] + [kx_ref[pl.ds(s, tq), :] for s in (st_p, st_m, st_n)], axis=0)
        v_all = jnp.concatenate([vc_ref[...]] + [vx_ref[pl.ds(s, tq), :] for s in (st_p, st_m, st_n)], axis=0)
        n_ctx = kc_ref.shape[0]
        r = lax.broadcasted_iota(jnp.int32, (tq, tq), 0)
        c = lax.broadcasted_iota(jnp.int32, (tq, tq), 1)
        zero = jnp.zeros((tq, tq), F32)
        bias_p = jnp.where(jnp.logical_and(c >= r, qb >= 1), 0.0, NEG_INF)
        bias_n = jnp.where(jnp.logical_and(c <= r, qb + 1 < nqb), 0.0, NEG_INF)
        bias = jnp.concatenate([jnp.zeros((tq, n_ctx), F32), bias_p, zero, bias_n], axis=1)
        bias = jnp.concatenate([bias] * tiles, axis=0)
    else:
        k_all = kc_ref[...]
        v_all = vc_ref[...]
        bias = None
    lo = lax.broadcasted_iota(jnp.int32, k_all.shape, 1) < half

    def half_variants(x):
        xf = x.astype(F32)
        xs = pltpu.roll(xf, half, 1)
        keep_lo = lambda a: jnp.where(lo, a, 0.0).astype(BF16)
        keep_hi = lambda a: jnp.where(lo, 0.0, a).astype(BF16)
        return [(keep_lo(xf), keep_hi(xs)), (keep_lo(xs), keep_hi(xf))]

    k_var = half_variants(k_all)
    v_var = half_variants(v_all)
    for gi in range(kv_heads):
        kz = k_var[gi]
        vz = v_var[gi]
        q_stack = jnp.concatenate(
            [q_ref[:, (gi * tiles + j) * LANES:(gi * tiles + j + 1) * LANES] for j in range(tiles)], axis=0)
        out = None
        for par in range(2):
            s = _dot_nt(q_stack, kz[par])
            if bias is not None:
                s = s + bias
            head0 = gp * heads_per_step + gi * group + par
            sk = jnp.concatenate(
                [jnp.full((tq, 1), sink_ref[head0 + 2 * j] * LOG2E, F32) for j in range(tiles)], axis=0)
            m = jnp.maximum(jnp.max(s, axis=-1, keepdims=True), sk)
            e = jnp.exp2(s - m)
            den = jnp.sum(e, axis=-1, keepdims=True) + jnp.exp2(sk - m)
            o = _dot(e.astype(BF16), vz[par]) * (1.0 / den)
            out = o if out is None else out + o
        for j in range(tiles):
            o_ref[:, (gi * tiles + j) * LANES:(gi * tiles + j + 1) * LANES] = out[j * tq:(j + 1) * tq].astype(BF16)


def _gw_kernel(*refs, nqb, has_ctx_steps, **kw):
    body = functools.partial(_gw_body, *refs, nqb=nqb, **kw)
    if has_ctx_steps:
        qb = pl.program_id(2)
        pl.when(qb < nqb)(lambda: body(has_x=True))
        pl.when(qb >= nqb)(lambda: body(has_x=False))
    else:
        body(has_x=True)


def _gw_attention(qkv, sinks, *, layer, batch, seq, n_ctx, need_ctx, d):
    t = qkv.shape[0]
    dim = d // GW_HEADS
    kv_per_step = LANES // dim
    hps = kv_per_step * GW_GROUP
    qw = hps * dim
    n_gp = GW_KV_HEADS // kv_per_step
    tq = WINDOW
    tc_rows = batch * n_ctx
    assert seq % tq == 0 and n_ctx % tq == 0 and tc_rows % seq == 0
    nqb = seq // tq
    ncq = n_ctx // tq if need_ctx else 0
    qoff = tc_rows // tq
    xoff = tc_rows // seq
    kcol = d // LANES
    vcol = kcol + GW_KV_HEADS * dim // LANES

    def q_row(b, qb):
        lat = qoff + b * nqb + jnp.minimum(qb, nqb - 1)
        return jnp.where(qb < nqb, lat, b * ncq + (qb - nqb)) if need_ctx else lat

    def o_row(b, qb):
        return q_row(b, qb) if need_ctx else b * nqb + qb

    return pl.pallas_call(
        functools.partial(_gw_kernel, nqb=nqb, has_ctx_steps=need_ctx, heads_per_step=hps, group=GW_GROUP, dim=dim),
        grid=(batch, n_gp, nqb + ncq),
        in_specs=[
            pl.BlockSpec(memory_space=pltpu.SMEM),
            pl.BlockSpec((tq, qw), lambda b, g, qb: (q_row(b, qb), g)),
            pl.BlockSpec((n_ctx, LANES), lambda b, g, qb: (b, kcol + g)),
            pl.BlockSpec((n_ctx, LANES), lambda b, g, qb: (b, vcol + g)),
            pl.BlockSpec((seq, LANES), lambda b, g, qb: (xoff + b, kcol + g)),
            pl.BlockSpec((seq, LANES), lambda b, g, qb: (xoff + b, vcol + g)),
        ],
        out_specs=pl.BlockSpec((tq, qw), lambda b, g, qb: (o_row(b, qb), g)),
        out_shape=jax.ShapeDtypeStruct((t if need_ctx else batch * seq, d), BF16),
        compiler_params=_cparams(3),
        name=f"gqa_attn_l{layer}",
    )(sinks, qkv, qkv, qkv, qkv, qkv)


def _pack_bf16_pairs(h):
    half = h.shape[1] // 2
    hi = lax.bitcast_convert_type(h[:, :half].astype(BF16).astype(F32), jnp.uint32)
    lo = lax.bitcast_convert_type(h[:, half:].astype(BF16).astype(F32), jnp.uint32)
    return hi | (lo >> 16)


def _unpack_bf16_pairs(u, dtype=BF16):
    hi = lax.bitcast_convert_type(u & jnp.uint32(0xFFFF0000), F32)
    lo = lax.bitcast_convert_type(u << 16, F32)
    return jnp.concatenate([hi, lo], axis=1).astype(dtype)


def _oproj_kernel(o_ref, w_ref, x_ref, g1_ref, sh_ref, sc_ref, gn_ref, rhi_ref, rlo_ref,
                  xo_ref, h_ref, lg_ref):
    xn = x_ref[...] + g1_ref[0] * _dot(o_ref[...], w_ref[...])
    xo_ref[...] = xn
    ms = jnp.mean(xn * xn, axis=-1, keepdims=True)
    h = xn * lax.rsqrt(ms + RMS_EPS) * gn_ref[...]
    h = h * (1.0 + sc_ref[0]) + sh_ref[0]
    h_ref[...] = _pack_bf16_pairs(h)
    h_hi = h.astype(BF16)
    h_lo = (h - h_hi.astype(F32)).astype(BF16)
    lg = _dot(h_hi, rhi_ref[...]) + (_dot(h_hi, rlo_ref[...]) + _dot(h_lo, rhi_ref[...]))
    lg_ref[...] = _route_rows(lg)


def _route_rows(lg):
    lane = lax.broadcasted_iota(jnp.int32, lg.shape, 1).astype(F32)
    far = jnp.float32(2 * ROUTER_LANES)

    def first_lane(cond):
        return jnp.min(jnp.where(cond, lane, far), axis=-1, keepdims=True)

    is_g = lane < N_GROUPS
    gl = jnp.where(is_g, lg, NEG_INF)
    gmax = jnp.max(gl, axis=-1, keepdims=True)
    g_sel = first_lane(jnp.logical_and(is_g, gl == gmax))
    g_w = 1.0 / jnp.sum(jnp.where(is_g, jnp.exp(gl - gmax), 0.0), axis=-1, keepdims=True)
    lo_e = N_GROUPS + g_sel * EXPERTS_PER_GROUP
    in_grp = jnp.logical_and(lane >= lo_e, lane < lo_e + EXPERTS_PER_GROUP)
    el = jnp.where(in_grp, lg, NEG_INF)
    ee = jnp.where(in_grp, jnp.exp(el - jnp.max(el, axis=-1, keepdims=True)), 0.0)
    prob = ee / jnp.sum(ee, axis=-1, keepdims=True)
    p1 = jnp.max(jnp.where(in_grp, prob, -1.0), axis=-1, keepdims=True)
    i1 = first_lane(jnp.logical_and(in_grp, prob == p1))
    rest = jnp.where(jnp.logical_and(in_grp, lane != i1), prob, -1.0)
    p2 = jnp.max(rest, axis=-1, keepdims=True)
    i2 = first_lane(jnp.logical_and(rest >= 0.0, rest == p2))
    den = p1 + p2
    vals = (g_w * (p1 / den), g_w * (p2 / den), i1 - N_GROUPS, i2 - N_GROUPS)
    out = jnp.zeros_like(lg)
    for k, v in enumerate(vals):
        out = jnp.where(lane == k, v, out)
    return out


def _oproj_call(o, w, xa, modr, gain, r_hi, r_lo, *, layer, n_ctx_rows, seq, latent_only):
    t, d = xa.shape
    tm = 256
    assert n_ctx_rows % tm == 0 and seq % tm == 0
    ncb = n_ctx_rows // tm
    bpb = seq // tm
    off = ncb if latent_only else 0
    nblk = t // tm - off
    base = layer * COND_ROWS

    def mrow(r, chunk):
        blk = r + off
        seg = jnp.where(blk < ncb, COND_ROWS // 2, (blk - ncb) // bpb)
        return ((base + seg) * N_MOD + chunk, 0, 0)

    const = lambda r: (0, 0)
    return pl.pallas_call(
        _oproj_kernel,
        grid=(nblk,),
        in_specs=[
            pl.BlockSpec((tm, d), lambda r: (r, 0)),
            pl.BlockSpec((d, d), const, pipeline_mode=pl.Buffered(1)),
            pl.BlockSpec((tm, d), lambda r: (r + off, 0)),
            pl.BlockSpec((1, 1, d), lambda r: mrow(r, 2)),
            pl.BlockSpec((1, 1, d), lambda r: mrow(r, 3)),
            pl.BlockSpec((1, 1, d), lambda r: mrow(r, 4)),
            pl.BlockSpec((1, d), const),
            pl.BlockSpec((d, ROUTER_LANES), const),
            pl.BlockSpec((d, ROUTER_LANES), const),
        ],
        out_specs=[
            pl.BlockSpec((tm, d), lambda r: (r + off, 0)),
            pl.BlockSpec((tm, d // 2), lambda r: (r, 0)),
            pl.BlockSpec((tm, ROUTER_LANES), lambda r: (r, 0)),
        ],
        out_shape=[
            jax.ShapeDtypeStruct((t, d), F32),
            jax.ShapeDtypeStruct((nblk * tm, d // 2), jnp.uint32),
            jax.ShapeDtypeStruct((nblk * tm, ROUTER_LANES), F32),
        ],
        input_output_aliases={2: 0},
        compiler_params=_cparams(1),
        name=f"out_proj_l{layer}",
    )(o, w, xa, modr, modr, modr, gain.reshape(1, d), r_hi, r_lo)


def _inclusive_cumsum_rows(m):
    n, k = m.shape
    blk = LANES
    assert n % blk == 0
    tri = (jnp.arange(blk)[:, None] >= jnp.arange(blk)[None, :]).astype(F32)
    m3 = m.reshape(n // blk, blk, k)
    within = jnp.einsum("ij,bjk->bik", tri, m3, precision=lax.Precision.HIGHEST)
    sums = within[:, -1, :]
    nb = n // blk
    tri_b = (jnp.arange(nb)[:, None] > jnp.arange(nb)[None, :]).astype(F32)
    offs = jnp.dot(tri_b, sums, precision=lax.Precision.HIGHEST)
    return (within + offs[:, None, :]).reshape(n, k)


def _route(routed, tb):
    t = routed.shape[0]
    w = routed[:, :TOP_K]
    eid = routed[:, TOP_K:2 * TOP_K].astype(jnp.int32).reshape(-1)
    a = t * TOP_K
    onehot = (eid[:, None] == jnp.arange(N_EXPERTS, dtype=jnp.int32)[None]).astype(F32)
    csum = _inclusive_cumsum_rows(onehot)
    counts = csum[-1].astype(jnp.int32)
    nblk = (counts + tb - 1) // tb
    bend = jnp.cumsum(nblk)
    bstart = bend - nblk
    slot = jnp.sum(onehot * (csum - 1.0 + (bstart * tb).astype(F32)[None]), axis=-1)
    dest = slot.astype(jnp.int32)
    n_blocks = -(-a // tb) + N_EXPERTS
    blocks = jnp.arange(n_blocks, dtype=jnp.int32)
    block_expert = jnp.minimum(jnp.sum((bend[None, :] <= blocks[:, None]).astype(jnp.int32), axis=1), N_EXPERTS - 1)
    nact = bend[-1:].astype(jnp.int32)
    sorted_tok = (jnp.argsort(eid, stable=True) // TOP_K).astype(jnp.int32)
    start = jnp.cumsum(counts) - counts
    slot_e = jnp.repeat(block_expert, tb)
    off = jnp.arange(n_blocks * tb, dtype=jnp.int32) - bstart[slot_e] * tb
    src = jnp.clip(start[slot_e] + off, 0, a - 1)
    slot_tok = jnp.where(off < counts[slot_e], sorted_tok[src], 0)
    return w, dest, slot_tok, block_expert, nact, n_blocks


GATHER_LAG = 2


def _moe_kernel(be_ref, nact_ref, st_ref, h_hbm, wg_ref, wu_ref, wd_ref, y_ref, xbuf, wg_s, wu_s, wd_s, sem):
    s = pl.program_id(0)
    nact = nact_ref[0]
    nbuf, tb = xbuf.shape[0], xbuf.shape[1]
    blk = s - GATHER_LAG
    do_gather = s < nact
    do_ffn = jnp.logical_and(blk >= 0, blk < nact)
    prev_blk = jnp.maximum(blk - 1, 0)
    fresh = jnp.logical_or(blk == 0, be_ref[jnp.maximum(blk, 0)] != be_ref[prev_blk])

    def gather_row(r, slot):
        pltpu.make_async_copy(h_hbm.at[pl.ds(st_ref[0, 0, r], 1)], xbuf.at[slot, pl.ds(r, 1)], sem.at[slot]).start()

    def gather_rows():
        for r in range(tb):
            gather_row(r, s % nbuf)

    def gather_rows_rolled():
        lax.fori_loop(0, tb, lambda r, c: (gather_row(r, s % nbuf), c)[1], 0, unroll=8)

    def ffn(also_gather):
        slot = blk % nbuf
        pltpu.make_async_copy(h_hbm.at[pl.ds(0, tb)], xbuf.at[slot], sem.at[slot]).wait()
        xb = _unpack_bf16_pairs(xbuf[slot])
        if also_gather:
            gather_rows()
        g = _dot(xb, wg_s[...])
        u = _dot(xb, wu_s[...])
        act = (g / (1.0 + jnp.exp(-g))) * u
        y_ref[...] = _pack_bf16_pairs(_dot(act.astype(BF16), wd_s[...]))

    @pl.when(jnp.logical_and(do_ffn, fresh))
    def _():
        wg_s[...] = wg_ref[...].astype(BF16)
        wu_s[...] = wu_ref[...].astype(BF16)
        wd_s[...] = wd_ref[...].astype(BF16)

    pl.when(jnp.logical_and(do_ffn, do_gather))(lambda: ffn(True))
    pl.when(jnp.logical_and(do_ffn, jnp.logical_not(do_gather)))(lambda: ffn(False))
    pl.when(jnp.logical_and(jnp.logical_not(do_ffn), do_gather))(gather_rows_rolled)

    @pl.when(jnp.logical_and(blk >= 0, blk >= nact))
    def _():
        y_ref[...] = jnp.zeros_like(y_ref)


def _moe_call(hp, slot_tok, block_expert, nact, wg, wu, wd, *, n_blocks, tb, layer):
    dw = hp.shape[1]
    d, de = wg.shape[2], wg.shape[3]
    prev = lambda s: jnp.maximum(s - GATHER_LAG, 0)
    grid_spec = pltpu.PrefetchScalarGridSpec(
        num_scalar_prefetch=2,
        grid=(n_blocks + GATHER_LAG,),
        in_specs=[
            pl.BlockSpec((1, 1, tb), lambda s, be, na: (jnp.minimum(s, n_blocks - 1), 0, 0),
                         memory_space=pltpu.SMEM),
            pl.BlockSpec(memory_space=pl.ANY),
            pl.BlockSpec((None, None, d, de), lambda s, be, na: (layer, be[prev(s)], 0, 0)),
            pl.BlockSpec((None, None, d, de), lambda s, be, na: (layer, be[prev(s)], 0, 0)),
            pl.BlockSpec((None, None, de, d), lambda s, be, na: (layer, be[prev(s)], 0, 0)),
        ],
        out_specs=pl.BlockSpec((tb, d // 2), lambda s, be, na: (prev(s), 0)),
        scratch_shapes=[pltpu.VMEM((GATHER_LAG + 1, tb, dw), jnp.uint32), pltpu.VMEM((d, de), BF16),
                        pltpu.VMEM((d, de), BF16), pltpu.VMEM((de, d), BF16),
                        pltpu.SemaphoreType.DMA((GATHER_LAG + 1,))],
    )
    return pl.pallas_call(
        _moe_kernel,
        grid_spec=grid_spec,
        out_shape=jax.ShapeDtypeStruct((n_blocks * tb, d // 2), jnp.uint32),
        compiler_params=_cparams(1),
        name=f"moe_experts_l{layer}",
    )(block_expert, nact, slot_tok.reshape(n_blocks, 1, tb), hp, wg, wu, wd)


def _row_gather_start(idx_ref, n_rows, src_hbm, dst, sem, *, idx_stride=1, idx_off=0):
    def body(r, carry):
        tok = idx_ref[0, 0, r * idx_stride + idx_off]
        pltpu.make_async_copy(src_hbm.at[pl.ds(tok, 1)], dst.at[pl.ds(r, 1)], sem).start()
        return carry
    lax.fori_loop(0, n_rows, body, 0, unroll=8)


def _row_gather_wait(n_rows, src_hbm, dst, sem):
    pltpu.make_async_copy(src_hbm.at[pl.ds(0, n_rows)], dst, sem).wait()


def _combine_kernel(pos_ref, y_hbm, w_ref, x_ref, g2_ref, gout_ref, o_ref, ybuf, sem, *, final_norm):
    s = pl.program_id(0)
    nb = pl.num_programs(0) - 1
    tc = ybuf.shape[2]

    @pl.when(s < nb)
    def _():
        slot = s % 2
        for k in range(TOP_K):
            _row_gather_start(pos_ref, tc, y_hbm, ybuf.at[slot, k], sem.at[slot], idx_stride=TOP_K, idx_off=k)

    @pl.when(s >= 1)
    def _():
        slot = (s - 1) % 2
        for k in range(TOP_K):
            _row_gather_wait(tc, y_hbm, ybuf.at[slot, k], sem.at[slot])
        w = w_ref[...]
        moe = (_unpack_bf16_pairs(ybuf[slot, 0], F32) * w[:, 0:1]
               + _unpack_bf16_pairs(ybuf[slot, 1], F32) * w[:, 1:2])
        xn = x_ref[...] + g2_ref[0] * moe
        if final_norm:
            ms = jnp.mean(xn * xn, axis=-1, keepdims=True)
            xn = xn * lax.rsqrt(ms + RMS_EPS) * gout_ref[...]
        o_ref[...] = xn


def _combine_call(y, dest, w, xa, modr, gout, *, layer, n_ctx_rows, seq, latent_only, final_norm):
    t, d = xa.shape
    tc = 256
    assert n_ctx_rows % tc == 0 and seq % tc == 0
    ncb = n_ctx_rows // tc
    bpb = seq // tc
    off = ncb if latent_only else 0
    nblk = t // tc - off
    base = layer * COND_ROWS
    prev = lambda s: jnp.maximum(s - 1, 0)

    def mrow(s):
        blk = prev(s) + off
        seg = jnp.where(blk < ncb, COND_ROWS // 2, (blk - ncb) // bpb)
        return ((base + seg) * N_MOD + 5, 0, 0)

    out_off = 0 if final_norm else off
    out_rows = nblk * tc if final_norm else t
    return pl.pallas_call(
        functools.partial(_combine_kernel, final_norm=final_norm),
        grid=(nblk + 1,),
        in_specs=[
            pl.BlockSpec((1, 1, TOP_K * tc), lambda s: (jnp.minimum(s, nblk - 1), 0, 0), memory_space=pltpu.SMEM),
            pl.BlockSpec(memory_space=pl.ANY),
            pl.BlockSpec((tc, TOP_K), lambda s: (prev(s), 0)),
            pl.BlockSpec((tc, d), lambda s: (prev(s) + off, 0)),
            pl.BlockSpec((1, 1, d), mrow),
            pl.BlockSpec((1, d), lambda s: (0, 0)),
        ],
        out_specs=pl.BlockSpec((tc, d), lambda s: (prev(s) + out_off, 0)),
        out_shape=jax.ShapeDtypeStruct((out_rows, d), F32),
        scratch_shapes=[pltpu.VMEM((2, TOP_K, tc, y.shape[1]), y.dtype), pltpu.SemaphoreType.DMA((2,))],
        input_output_aliases={} if final_norm else {3: 0},
        compiler_params=_cparams(1),
        name=f"moe_combine_l{layer}",
    )(dest.reshape(nblk, 1, TOP_K * tc), y, w, xa, modr, gout.reshape(1, d))


def kernel(x, c, ctx, c_ctx, ada_w, ada_b, norm_mix_g, norm_ffn_g, norm_out_g, da_w_in, da_w_out,
           da_lam_q1, da_lam_k1, da_lam_q2, da_lam_k2, da_subln_g, gw_w_in, gw_w_out, gw_sinks,
           moe_w_group, moe_w_expert, moe_w_gate, moe_w_up, moe_w_down):
    batch, seq, d = x.shape
    n_ctx = ctx.shape[1]
    tc_rows = batch * n_ctx
    depth = ada_w.shape[0]
    assert batch < COND_ROWS // 2 + 1 and depth == DEPTH

    xa = jnp.concatenate([ctx.reshape(tc_rows, d), x.reshape(batch * seq, d)], axis=0)
    cond = jnp.zeros((COND_ROWS, d), F32).at[:batch].set(c).at[COND_ROWS // 2].set(c_ctx)
    mod = _ada_call(cond, ada_w, ada_b)
    modr = mod.reshape(depth * COND_ROWS * N_MOD, 1, d)

    da_dim = d // (2 * DA_HEADS)
    gw_dim = d // GW_HEADS
    tb = 256
    out = None
    for i in range(depth):
        last = i == depth - 1
        need_ctx = not last
        jm = i // N_MIXERS
        if i % N_MIXERS == 0:
            qkv = _qkv_call(xa, modr, norm_mix_g[i], da_w_in[jm].astype(BF16), layer=i,
                            n_ctx_rows=tc_rows, seq=seq, n_rope_cols=2 * d, n_q_cols=d,
                            qscale=da_dim ** -0.5 * LOG2E, rope_dim=da_dim)
            o = _da_attention(qkv, (da_lam_q1[jm], da_lam_k1[jm], da_lam_q2[jm], da_lam_k2[jm]),
                              da_subln_g[jm], layer=i, batch=batch, seq=seq, n_ctx=n_ctx, need_ctx=need_ctx)
            w_out = da_w_out[jm]
        else:
            kv_w = GW_KV_HEADS * gw_dim
            qkv = _qkv_call(xa, modr, norm_mix_g[i], gw_w_in[jm].astype(BF16), layer=i,
                            n_ctx_rows=tc_rows, seq=seq, n_rope_cols=d + kv_w, n_q_cols=d,
                            qscale=gw_dim ** -0.5 * LOG2E, rope_dim=gw_dim)
            o = _gw_attention(qkv, gw_sinks[jm], layer=i, batch=batch, seq=seq, n_ctx=n_ctx,
                              need_ctx=need_ctx, d=d)
            w_out = gw_w_out[jm]
        w_r = jnp.concatenate([moe_w_group[i], moe_w_expert[i]], axis=1)
        w_r = jnp.pad(w_r, ((0, 0), (0, ROUTER_LANES - w_r.shape[1])))
        r_hi = w_r.astype(BF16)
        r_lo = (w_r - r_hi.astype(F32)).astype(BF16)
        xa, h2, logits = _oproj_call(o, w_out.astype(BF16), xa, modr, norm_ffn_g[i], r_hi, r_lo, layer=i,
                                     n_ctx_rows=tc_rows, seq=seq, latent_only=last)
        w_tok, dest, slot_tok, block_expert, nact, n_blocks = _route(logits, tb)
        y = _moe_call(h2, slot_tok, block_expert, nact, moe_w_gate, moe_w_up, moe_w_down,
                      n_blocks=n_blocks, tb=tb, layer=i)
        res = _combine_call(y, dest, w_tok, xa, modr, norm_out_g, layer=i, n_ctx_rows=tc_rows, seq=seq,
                            latent_only=last, final_norm=last)
        if last:
            out = res
        else:
            xa = res
    return out.reshape(batch, seq, d)
```

```python
import functools
import math

import jax
import jax.numpy as jnp
from jax import lax
from jax.experimental import pallas as pl
from jax.experimental.pallas import tpu as pltpu

F32 = jnp.float32
BF16 = jnp.bfloat16

DEPTH = 4
N_MIXERS = 2
GRID_W = 64
DA_HEADS = 8
GW_HEADS = 32
GW_KV_HEADS = 4
GW_GROUP = GW_HEADS // GW_KV_HEADS
WINDOW = 128
N_GROUPS = 4
EXPERTS_PER_GROUP = 8
N_EXPERTS = N_GROUPS * EXPERTS_PER_GROUP
TOP_K = 2
ROPE_BASE = 10000.0
RMS_EPS = 1e-6
NEG_INF = -1e30
LOG2E = math.log2(math.e)

LANES = 128
COND_ROWS = 32
N_MOD = 6
ROUTER_LANES = 128
EPILOGUE_ROWS = 256
VMEM_LIMIT = 56 * 1024 * 1024


def _cparams(n_axes, vmem=VMEM_LIMIT):
    return pltpu.CompilerParams(dimension_semantics=("arbitrary",) * n_axes, vmem_limit_bytes=vmem)


def _dot(a, b):
    return jnp.dot(a, b, preferred_element_type=F32)


def _dot_nt(a, b):
    return lax.dot_general(a, b, (((1,), (1,)), ((), ())), preferred_element_type=F32)


def _ada_kernel(c_ref, w_ref, b_ref, o_ref):
    c = c_ref[...]
    s = (c / (1.0 + jnp.exp(-c))).astype(BF16)
    o_ref[...] = _dot(s, w_ref[...].astype(BF16)) + b_ref[...]


def _ada_call(cond, ada_w, ada_b):
    depth, d, n = ada_w.shape
    tn = 1024
    return pl.pallas_call(
        _ada_kernel,
        grid=(depth, n // tn),
        in_specs=[
            pl.BlockSpec((COND_ROWS, d), lambda l, j: (0, 0)),
            pl.BlockSpec((None, d, tn), lambda l, j: (l, 0, j)),
            pl.BlockSpec((None, 1, tn), lambda l, j: (l, 0, j)),
        ],
        out_specs=pl.BlockSpec((None, COND_ROWS, tn), lambda l, j: (l, 0, j)),
        out_shape=jax.ShapeDtypeStruct((depth, COND_ROWS, n), F32),
        compiler_params=_cparams(2),
        name="ada_mod",
    )(cond, ada_w, ada_b.reshape(depth, 1, n))


def _rope_tables(seq, dim):
    rows = seq // GRID_W
    row = jnp.repeat(jnp.arange(rows, dtype=F32), GRID_W)
    col = jnp.tile(jnp.arange(GRID_W, dtype=F32), rows)
    half = dim // 2
    inv = ROPE_BASE ** (-jnp.arange(0, half, 2, dtype=F32) / half)
    ang_r = row[:, None] * inv[None]
    ang_c = col[:, None] * inv[None]
    cr, sr, cc, sc = jnp.cos(ang_r), jnp.sin(ang_r), jnp.cos(ang_c), jnp.sin(ang_c)
    z = jnp.zeros_like(sr)
    reps = LANES // dim
    cos_t = jnp.tile(jnp.concatenate([cr, cr, cc, cc], axis=1), (1, reps))
    sin_a = jnp.tile(jnp.concatenate([-sr, z, -sc, z], axis=1), (1, reps))
    sin_b = jnp.tile(jnp.concatenate([z, sr, z, sc], axis=1), (1, reps))
    return cos_t, sin_a, sin_b


def _epilogue_tables(seq, dim, qscale, tm):
    cos_t, sin_a, sin_b = _rope_tables(seq, dim)
    one = jnp.ones((tm, LANES), F32)
    zero = jnp.zeros((tm, LANES), F32)
    cos_all = jnp.concatenate([cos_t * qscale, cos_t, one, one * qscale], axis=0)
    sa_all = jnp.concatenate([sin_a * qscale, sin_a, zero, zero], axis=0)
    sb_all = jnp.concatenate([sin_b * qscale, sin_b, zero, zero], axis=0)
    return cos_all, sa_all, sb_all


def _qkv_kernel(x_ref, sh_ref, sc_ref, g_ref, w_ref, cos_l, sa_l, sb_l, cos_r, sa_r, sb_r, o_ref, h_scr, *,
                rope_q, tn):
    n_tiles = tn // LANES
    tm = h_scr.shape[0]
    rows = min(tm, EPILOGUE_ROWS)

    def chunk(rc, first_col_block):
        rs = slice(rc * rows, (rc + 1) * rows)
        if first_col_block:
            x = x_ref[rs, :]
            ms = jnp.mean(x * x, axis=-1, keepdims=True)
            y = x * lax.rsqrt(ms + RMS_EPS) * g_ref[...]
            h = (y * (1.0 + sc_ref[0]) + sh_ref[0]).astype(BF16)
            h_scr[rs, :] = h
        else:
            h = h_scr[rs, :]
        acc = _dot(h, w_ref[...])
        for ci in range(n_tiles):
            cos_ref, sa_ref, sb_ref = (cos_l, sa_l, sb_l) if ci < n_tiles // 2 else (cos_r, sa_r, sb_r)
            a = acc[:, ci * LANES:(ci + 1) * LANES]
            a = (a * cos_ref[rs, :] + pltpu.roll(a, LANES - rope_q, 1) * sa_ref[rs, :]
                 + pltpu.roll(a, rope_q, 1) * sb_ref[rs, :])
            o_ref[rs, ci * LANES:(ci + 1) * LANES] = a.astype(BF16)

    def all_chunks(first_col_block):
        for rc in range(tm // rows):
            chunk(rc, first_col_block)

    j = pl.program_id(1)
    pl.when(j == 0)(lambda: all_chunks(True))
    pl.when(j != 0)(lambda: all_chunks(False))


def _qkv_call(xa, modr, gain, w, *, layer, n_ctx_rows, seq, n_rope_cols, n_q_cols, qscale, rope_dim):
    t, d = xa.shape
    n = w.shape[1]
    tn = 512
    hn = tn // 2
    tm = min(1024, seq, n_ctx_rows)
    assert t % tm == 0 and n_ctx_rows % tm == 0 and seq % tm == 0 and n % tn == 0
    assert n_q_cols % hn == 0 and n_rope_cols % hn == 0
    ncb = n_ctx_rows // tm
    bpb = seq // tm
    base = layer * COND_ROWS
    tables = _epilogue_tables(seq, rope_dim, qscale, tm)

    def seg(r):
        return jnp.where(r < ncb, COND_ROWS // 2, (r - ncb) // bpb)

    def tab_spec(half):
        def idx(r, j):
            col0 = j * tn + half * hn
            p = (r - ncb) % bpb
            lat = jnp.where(col0 < n_q_cols, p, jnp.where(col0 < n_rope_cols, bpb + p, 2 * bpb))
            ctx = jnp.where(col0 < n_q_cols, 2 * bpb + 1, 2 * bpb)
            return (jnp.where(r >= ncb, lat, ctx), 0)
        return pl.BlockSpec((tm, LANES), idx)

    return pl.pallas_call(
        functools.partial(_qkv_kernel, rope_q=rope_dim // 4, tn=tn),
        grid=(t // tm, n // tn),
        in_specs=[
            pl.BlockSpec((tm, d), lambda r, j: (r, 0)),
            pl.BlockSpec((1, 1, d), lambda r, j: ((base + seg(r)) * N_MOD + 0, 0, 0)),
            pl.BlockSpec((1, 1, d), lambda r, j: ((base + seg(r)) * N_MOD + 1, 0, 0)),
            pl.BlockSpec((1, d), lambda r, j: (0, 0)),
            pl.BlockSpec((d, tn), lambda r, j: (0, j)),
        ] + [tab_spec(0)] * 3 + [tab_spec(1)] * 3,
        out_specs=pl.BlockSpec((tm, tn), lambda r, j: (r, j)),
        out_shape=jax.ShapeDtypeStruct((t, n), BF16),
        scratch_shapes=[pltpu.VMEM((tm, d), BF16)],
        compiler_params=_cparams(2),
        name=f"qkv_proj_l{layer}",
    )(xa, modr, modr, gain.reshape(1, d), w, *tables, *tables)


def _da_body(q_ref, kc_ref, vc_ref, kx_ref, vx_ref, lq1, lk1, lq2, lk2, g_ref, o_ref, *, lam_init, has_x):
    dh = q_ref.shape[1] // 2
    lam = (jnp.exp(jnp.sum(lq1[...] * lk1[...], axis=-1, keepdims=True))
           - jnp.exp(jnp.sum(lq2[...] * lk2[...], axis=-1, keepdims=True)) + lam_init)

    def probs(lo):
        qc = q_ref[:, lo:lo + dh]
        s_c = _dot_nt(qc, kc_ref[:, lo:lo + dh])
        m = jnp.max(s_c, axis=-1, keepdims=True)
        if has_x:
            s_x = _dot_nt(qc, kx_ref[:, lo:lo + dh])
            m = jnp.maximum(m, jnp.max(s_x, axis=-1, keepdims=True))
        e_c = jnp.exp2(s_c - m)
        den = jnp.sum(e_c, axis=-1, keepdims=True)
        e_x = None
        if has_x:
            e_x = jnp.exp2(s_x - m)
            den = den + jnp.sum(e_x, axis=-1, keepdims=True)
            e_x = e_x.astype(BF16)
        return e_c.astype(BF16), e_x, 1.0 / den

    e1c, e1x, r1 = probs(0)
    e2c, e2x, r2 = probs(dh)
    tq = q_ref.shape[0]
    ov = _dot(jnp.concatenate([e1c, e2c], axis=0), vc_ref[...])
    if has_x:
        ov = ov + _dot(jnp.concatenate([e1x, e2x], axis=0), vx_ref[...])
    o = ov[:tq] * r1 - ov[tq:] * (lam * r2)
    ms = jnp.mean(o * o, axis=-1, keepdims=True)
    o = o * lax.rsqrt(ms + RMS_EPS) * g_ref[...] * (1.0 - lam_init)
    o_ref[...] = o.astype(BF16)


def _da_kernel(*refs, lam_init, nqb, has_ctx_steps):
    body = functools.partial(_da_body, *refs, lam_init=lam_init)
    if has_ctx_steps:
        qb = pl.program_id(2)
        pl.when(qb < nqb)(lambda: body(has_x=True))
        pl.when(qb >= nqb)(lambda: body(has_x=False))
    else:
        body(has_x=True)


def _da_attention(qkv, lam_vecs, subln_g, *, layer, batch, seq, n_ctx, need_ctx):
    t, n3 = qkv.shape
    d = n3 // 3
    hw = d // DA_HEADS
    tc_rows = batch * n_ctx
    tq = min(256, seq)
    assert seq % tq == 0 and n_ctx % tq == 0 and tc_rows % seq == 0 and hw % LANES == 0
    lam_init = 0.8 - 0.6 * math.exp(-0.3 * layer)
    nh = DA_HEADS
    small = [v.reshape(1, -1) for v in lam_vecs] + [subln_g.reshape(1, hw)]
    small_specs = [pl.BlockSpec(v.shape, lambda b, h, qb: (0, 0)) for v in small]
    qoff = tc_rows // tq
    nqb = seq // tq
    ncq = n_ctx // tq if need_ctx else 0
    xoff = tc_rows // seq

    def q_row(b, qb):
        lat = qoff + b * nqb + jnp.minimum(qb, nqb - 1)
        return jnp.where(qb < nqb, lat, b * ncq + (qb - nqb)) if need_ctx else lat

    def o_row(b, qb):
        return q_row(b, qb) if need_ctx else b * nqb + qb

    return pl.pallas_call(
        functools.partial(_da_kernel, lam_init=lam_init, nqb=nqb, has_ctx_steps=need_ctx),
        grid=(batch, nh, nqb + ncq),
        in_specs=[
            pl.BlockSpec((tq, hw), lambda b, h, qb: (q_row(b, qb), h)),
            pl.BlockSpec((n_ctx, hw), lambda b, h, qb: (b, nh + h)),
            pl.BlockSpec((n_ctx, hw), lambda b, h, qb: (b, 2 * nh + h)),
            pl.BlockSpec((seq, hw), lambda b, h, qb: (xoff + b, nh + h)),
            pl.BlockSpec((seq, hw), lambda b, h, qb: (xoff + b, 2 * nh + h)),
        ] + small_specs,
        out_specs=pl.BlockSpec((tq, hw), lambda b, h, qb: (o_row(b, qb), h)),
        out_shape=jax.ShapeDtypeStruct((t if need_ctx else batch * seq, d), BF16),
        compiler_params=_cparams(3),
        name=f"diff_attn_l{layer}",
    )(qkv, qkv, qkv, qkv, qkv, *small)


def _gw_body(sink_ref, q_ref, kc_ref, vc_ref, kx_ref, vx_ref, o_ref, *, has_x, heads_per_step, group, dim, nqb):
    gp = pl.program_id(1)
    tq = q_ref.shape[0]
    half = LANES // 2
    assert dim == half
    tiles = group * dim // LANES
    kv_heads = heads_per_step // group
    if has_x:
        qb = pl.program_id(2)
        st_p = pl.multiple_of(jnp.maximum(qb - 1, 0) * tq, tq)
        st_m = pl.multiple_of(qb * tq, tq)
        st_n = pl.multiple_of(jnp.minimum(qb + 1, nqb - 1) * tq, tq)
        k_all = jnp.concatenate([kc_ref[...]] + [kx_ref[pl.ds(s, tq), :] for s in (st_p, st_m, st_n)], axis=0)
        v_all = jnp.concatenate([vc_ref[...]] + [vx_ref[pl.ds(s, tq), :] for s in (st_p, st_m, st_n)], axis=0)
        n_ctx = kc_ref.shape[0]
        r = lax.broadcasted_iota(jnp.int32, (tq, tq), 0)
        c = lax.broadcasted_iota(jnp.int32, (tq, tq), 1)
        zero = jnp.zeros((tq, tq), F32)
        bias_p = jnp.where(jnp.logical_and(c >= r, qb >= 1), 0.0, NEG_INF)
        bias_n = jnp.where(jnp.logical_and(c <= r, qb + 1 < nqb), 0.0, NEG_INF)
        bias = jnp.concatenate([jnp.zeros((tq, n_ctx), F32), bias_p, zero, bias_n], axis=1)
        bias = jnp.concatenate([bias] * tiles, axis=0)
    else:
        k_all = kc_ref[...]
        v_all = vc_ref[...]
        bias = None
    lo = lax.broadcasted_iota(jnp.int32, k_all.shape, 1) < half

    def half_variants(x):
        xf = x.astype(F32)
        xs = pltpu.roll(xf, half, 1)
        keep_lo = lambda a: jnp.where(lo, a, 0.0).astype(BF16)
        keep_hi = lambda a: jnp.where(lo, 0.0, a).astype(BF16)
        return [(keep_lo(xf), keep_hi(xs)), (keep_lo(xs), keep_hi(xf))]

    k_var = half_variants(k_all)
    v_var = half_variants(v_all)
    for gi in range(kv_heads):
        kz = k_var[gi]
        vz = v_var[gi]
        q_stack = jnp.concatenate(
            [q_ref[:, (gi * tiles + j) * LANES:(gi * tiles + j + 1) * LANES] for j in range(tiles)], axis=0)
        out = None
        for par in range(2):
            s = _dot_nt(q_stack, kz[par])
            if bias is not None:
                s = s + bias
            head0 = gp * heads_per_step + gi * group + par
            sk = jnp.concatenate(
                [jnp.full((tq, 1), sink_ref[head0 + 2 * j] * LOG2E, F32) for j in range(tiles)], axis=0)
            m = jnp.maximum(jnp.max(s, axis=-1, keepdims=True), sk)
            e = jnp.exp2(s - m)
            den = jnp.sum(e, axis=-1, keepdims=True) + jnp.exp2(sk - m)
            o = _dot(e.astype(BF16), vz[par]) * (1.0 / den)
            out = o if out is None else out + o
        for j in range(tiles):
            o_ref[:, (gi * tiles + j) * LANES:(gi * tiles + j + 1) * LANES] = out[j * tq:(j + 1) * tq].astype(BF16)


def _gw_kernel(*refs, nqb, has_ctx_steps, **kw):
    body = functools.partial(_gw_body, *refs, nqb=nqb, **kw)
    if has_ctx_steps:
        qb = pl.program_id(2)
        pl.when(qb < nqb)(lambda: body(has_x=True))
        pl.when(qb >= nqb)(lambda: body(has_x=False))
    else:
        body(has_x=True)


def _gw_attention(qkv, sinks, *, layer, batch, seq, n_ctx, need_ctx, d):
    t = qkv.shape[0]
    dim = d // GW_HEADS
    kv_per_step = LANES // dim
    hps = kv_per_step * GW_GROUP
    qw = hps * dim
    n_gp = GW_KV_HEADS // kv_per_step
    tq = WINDOW
    tc_rows = batch * n_ctx
    assert seq % tq == 0 and n_ctx % tq == 0 and tc_rows % seq == 0
    nqb = seq // tq
    ncq = n_ctx // tq if need_ctx else 0
    qoff = tc_rows // tq
    xoff = tc_rows // seq
    kcol = d // LANES
    vcol = kcol + GW_KV_HEADS * dim // LANES

    def q_row(b, qb):
        lat = qoff + b * nqb + jnp.minimum(qb, nqb - 1)
        return jnp.where(qb < nqb, lat, b * ncq + (qb - nqb)) if need_ctx else lat

    def o_row(b, qb):
        return q_row(b, qb) if need_ctx else b * nqb + qb

    return pl.pallas_call(
        functools.partial(_gw_kernel, nqb=nqb, has_ctx_steps=need_ctx, heads_per_step=hps, group=GW_GROUP, dim=dim),
        grid=(batch, n_gp, nqb + ncq),
        in_specs=[
            pl.BlockSpec(memory_space=pltpu.SMEM),
            pl.BlockSpec((tq, qw), lambda b, g, qb: (q_row(b, qb), g)),
            pl.BlockSpec((n_ctx, LANES), lambda b, g, qb: (b, kcol + g)),
            pl.BlockSpec((n_ctx, LANES), lambda b, g, qb: (b, vcol + g)),
            pl.BlockSpec((seq, LANES), lambda b, g, qb: (xoff + b, kcol + g)),
            pl.BlockSpec((seq, LANES), lambda b, g, qb: (xoff + b, vcol + g)),
        ],
        out_specs=pl.BlockSpec((tq, qw), lambda b, g, qb: (o_row(b, qb), g)),
        out_shape=jax.ShapeDtypeStruct((t if need_ctx else batch * seq, d), BF16),
        compiler_params=_cparams(3),
        name=f"gqa_attn_l{layer}",
    )(sinks, qkv, qkv, qkv, qkv, qkv)


def _pack_bf16_pairs(h):
    half = h.shape[1] // 2
    hi = lax.bitcast_convert_type(h[:, :half].astype(BF16).astype(F32), jnp.uint32)
    lo = lax.bitcast_convert_type(h[:, half:].astype(BF16).astype(F32), jnp.uint32)
    return hi | (lo >> 16)


def _unpack_bf16_pairs(u, dtype=BF16):
    hi = lax.bitcast_convert_type(u & jnp.uint32(0xFFFF0000), F32)
    lo = lax.bitcast_convert_type(u << 16, F32)
    return jnp.concatenate([hi, lo], axis=1).astype(dtype)


def _oproj_kernel(o_ref, w_ref, x_ref, g1_ref, sh_ref, sc_ref, gn_ref, rhi_ref, rlo_ref,
                  xo_ref, h_ref, lg_ref):
    tm = o_ref.shape[0]
    rows = min(tm, EPILOGUE_ROWS)
    for rc in range(tm // rows):
        rs = slice(rc * rows, (rc + 1) * rows)
        xn = x_ref[rs, :] + g1_ref[0] * _dot(o_ref[rs, :], w_ref[...])
        xo_ref[rs, :] = xn
        ms = jnp.mean(xn * xn, axis=-1, keepdims=True)
        h = xn * lax.rsqrt(ms + RMS_EPS) * gn_ref[...]
        h = h * (1.0 + sc_ref[0]) + sh_ref[0]
        h_ref[rs, :] = _pack_bf16_pairs(h)
        h_hi = h.astype(BF16)
        h_lo = (h - h_hi.astype(F32)).astype(BF16)
        lg = _dot(h_hi, rhi_ref[...]) + (_dot(h_hi, rlo_ref[...]) + _dot(h_lo, rhi_ref[...]))
        lg_ref[rs, :] = _route_rows(lg)


def _route_rows(lg):
    lane = lax.broadcasted_iota(jnp.int32, lg.shape, 1).astype(F32)
    far = jnp.float32(2 * ROUTER_LANES)

    def first_lane(cond):
        return jnp.min(jnp.where(cond, lane, far), axis=-1, keepdims=True)

    is_g = lane < N_GROUPS
    gl = jnp.where(is_g, lg, NEG_INF)
    gmax = jnp.max(gl, axis=-1, keepdims=True)
    g_sel = first_lane(jnp.logical_and(is_g, gl == gmax))
    g_w = 1.0 / jnp.sum(jnp.where(is_g, jnp.exp(gl - gmax), 0.0), axis=-1, keepdims=True)
    lo_e = N_GROUPS + g_sel * EXPERTS_PER_GROUP
    in_grp = jnp.logical_and(lane >= lo_e, lane < lo_e + EXPERTS_PER_GROUP)
    el = jnp.where(in_grp, lg, NEG_INF)
    ee = jnp.where(in_grp, jnp.exp(el - jnp.max(el, axis=-1, keepdims=True)), 0.0)
    prob = ee / jnp.sum(ee, axis=-1, keepdims=True)
    p1 = jnp.max(jnp.where(in_grp, prob, -1.0), axis=-1, keepdims=True)
    i1 = first_lane(jnp.logical_and(in_grp, prob == p1))
    rest = jnp.where(jnp.logical_and(in_grp, lane != i1), prob, -1.0)
    p2 = jnp.max(rest, axis=-1, keepdims=True)
    i2 = first_lane(jnp.logical_and(rest >= 0.0, rest == p2))
    den = p1 + p2
    vals = (g_w * (p1 / den), g_w * (p2 / den), i1 - N_GROUPS, i2 - N_GROUPS)
    out = jnp.zeros_like(lg)
    for k, v in enumerate(vals):
        out = jnp.where(lane == k, v, out)
    return out


def _oproj_call(o, w, xa, modr, gain, r_hi, r_lo, *, layer, n_ctx_rows, seq, latent_only):
    t, d = xa.shape
    tm = 512
    assert n_ctx_rows % tm == 0 and seq % tm == 0
    ncb = n_ctx_rows // tm
    bpb = seq // tm
    off = ncb if latent_only else 0
    nblk = t // tm - off
    base = layer * COND_ROWS

    def mrow(r, chunk):
        blk = r + off
        seg = jnp.where(blk < ncb, COND_ROWS // 2, (blk - ncb) // bpb)
        return ((base + seg) * N_MOD + chunk, 0, 0)

    const = lambda r: (0, 0)
    return pl.pallas_call(
        _oproj_kernel,
        grid=(nblk,),
        in_specs=[
            pl.BlockSpec((tm, d), lambda r: (r, 0)),
            pl.BlockSpec((d, d), const, pipeline_mode=pl.Buffered(1)),
            pl.BlockSpec((tm, d), lambda r: (r + off, 0)),
            pl.BlockSpec((1, 1, d), lambda r: mrow(r, 2)),
            pl.BlockSpec((1, 1, d), lambda r: mrow(r, 3)),
            pl.BlockSpec((1, 1, d), lambda r: mrow(r, 4)),
            pl.BlockSpec((1, d), const),
            pl.BlockSpec((d, ROUTER_LANES), const),
            pl.BlockSpec((d, ROUTER_LANES), const),
        ],
        out_specs=[
            pl.BlockSpec((tm, d), lambda r: (r + off, 0)),
            pl.BlockSpec((tm, d // 2), lambda r: (r, 0)),
            pl.BlockSpec((tm, ROUTER_LANES), lambda r: (r, 0)),
        ],
        out_shape=[
            jax.ShapeDtypeStruct((t, d), F32),
            jax.ShapeDtypeStruct((nblk * tm, d // 2), jnp.uint32),
            jax.ShapeDtypeStruct((nblk * tm, ROUTER_LANES), F32),
        ],
        input_output_aliases={2: 0},
        compiler_params=_cparams(1),
        name=f"out_proj_l{layer}",
    )(o, w, xa, modr, modr, modr, gain.reshape(1, d), r_hi, r_lo)


def _inclusive_cumsum_rows(m):
    n, k = m.shape
    blk = LANES
    assert n % blk == 0
    tri = (jnp.arange(blk)[:, None] >= jnp.arange(blk)[None, :]).astype(F32)
    m3 = m.reshape(n // blk, blk, k)
    within = jnp.einsum("ij,bjk->bik", tri, m3, precision=lax.Precision.HIGHEST)
    sums = within[:, -1, :]
    nb = n // blk
    tri_b = (jnp.arange(nb)[:, None] > jnp.arange(nb)[None, :]).astype(F32)
    offs = jnp.dot(tri_b, sums, precision=lax.Precision.HIGHEST)
    return (within + offs[:, None, :]).reshape(n, k)


def _route(routed, tb):
    t = routed.shape[0]
    w = routed[:, :TOP_K]
    eid = routed[:, TOP_K:2 * TOP_K].astype(jnp.int32).reshape(-1)
    a = t * TOP_K
    onehot = (eid[:, None] == jnp.arange(N_EXPERTS, dtype=jnp.int32)[None]).astype(F32)
    csum = _inclusive_cumsum_rows(onehot)
    counts = csum[-1].astype(jnp.int32)
    nblk = (counts + tb - 1) // tb
    bend = jnp.cumsum(nblk)
    bstart = bend - nblk
    slot = jnp.sum(onehot * (csum - 1.0 + (bstart * tb).astype(F32)[None]), axis=-1)
    dest = slot.astype(jnp.int32)
    n_blocks = -(-a // tb) + N_EXPERTS
    blocks = jnp.arange(n_blocks, dtype=jnp.int32)
    block_expert = jnp.minimum(jnp.sum((bend[None, :] <= blocks[:, None]).astype(jnp.int32), axis=1), N_EXPERTS - 1)
    nact = bend[-1:].astype(jnp.int32)
    sorted_tok = (jnp.argsort(eid, stable=True) // TOP_K).astype(jnp.int32)
    start = jnp.cumsum(counts) - counts
    slot_e = jnp.repeat(block_expert, tb)
    off = jnp.arange(n_blocks * tb, dtype=jnp.int32) - bstart[slot_e] * tb
    src = jnp.clip(start[slot_e] + off, 0, a - 1)
    slot_tok = jnp.where(off < counts[slot_e], sorted_tok[src], 0)
    return w, dest, slot_tok, block_expert, nact, n_blocks


GATHER_LAG = 2


def _moe_kernel(be_ref, nact_ref, st_ref, h_hbm, wg_ref, wu_ref, wd_ref, y_ref, xbuf, wg_s, wu_s, wd_s, sem):
    s = pl.program_id(0)
    nact = nact_ref[0]
    nbuf, tb = xbuf.shape[0], xbuf.shape[1]
    blk = s - GATHER_LAG
    do_gather = s < nact
    do_ffn = jnp.logical_and(blk >= 0, blk < nact)
    prev_blk = jnp.maximum(blk - 1, 0)
    fresh = jnp.logical_or(blk == 0, be_ref[jnp.maximum(blk, 0)] != be_ref[prev_blk])

    def gather_row(r, slot):
        pltpu.make_async_copy(h_hbm.at[pl.ds(st_ref[0, 0, r], 1)], xbuf.at[slot, pl.ds(r, 1)], sem.at[slot]).start()

    def gather_rows():
        for r in range(tb):
            gather_row(r, s % nbuf)

    def gather_rows_rolled():
        lax.fori_loop(0, tb, lambda r, c: (gather_row(r, s % nbuf), c)[1], 0, unroll=8)

    def ffn(also_gather):
        slot = blk % nbuf
        pltpu.make_async_copy(h_hbm.at[pl.ds(0, tb)], xbuf.at[slot], sem.at[slot]).wait()
        xb = _unpack_bf16_pairs(xbuf[slot])
        if also_gather:
            gather_rows()
        g = _dot(xb, wg_s[...])
        u = _dot(xb, wu_s[...])
        act = (g / (1.0 + jnp.exp(-g))) * u
        y_ref[...] = _pack_bf16_pairs(_dot(act.astype(BF16), wd_s[...]))

    @pl.when(jnp.logical_and(do_ffn, fresh))
    def _():
        wg_s[...] = wg_ref[...].astype(BF16)
        wu_s[...] = wu_ref[...].astype(BF16)
        wd_s[...] = wd_ref[...].astype(BF16)

    pl.when(jnp.logical_and(do_ffn, do_gather))(lambda: ffn(True))
    pl.when(jnp.logical_and(do_ffn, jnp.logical_not(do_gather)))(lambda: ffn(False))
    pl.when(jnp.logical_and(jnp.logical_not(do_ffn), do_gather))(gather_rows_rolled)

    @pl.when(jnp.logical_and(blk >= 0, blk >= nact))
    def _():
        y_ref[...] = jnp.zeros_like(y_ref)


def _moe_call(hp, slot_tok, block_expert, nact, wg, wu, wd, *, n_blocks, tb, layer):
    dw = hp.shape[1]
    d, de = wg.shape[2], wg.shape[3]
    prev = lambda s: jnp.maximum(s - GATHER_LAG, 0)
    grid_spec = pltpu.PrefetchScalarGridSpec(
        num_scalar_prefetch=2,
        grid=(n_blocks + GATHER_LAG,),
        in_specs=[
            pl.BlockSpec((1, 1, tb), lambda s, be, na: (jnp.minimum(s, n_blocks - 1), 0, 0),
                         memory_space=pltpu.SMEM),
            pl.BlockSpec(memory_space=pl.ANY),
            pl.BlockSpec((None, None, d, de), lambda s, be, na: (layer, be[prev(s)], 0, 0)),
            pl.BlockSpec((None, None, d, de), lambda s, be, na: (layer, be[prev(s)], 0, 0)),
            pl.BlockSpec((None, None, de, d), lambda s, be, na: (layer, be[prev(s)], 0, 0)),
        ],
        out_specs=pl.BlockSpec((tb, d // 2), lambda s, be, na: (prev(s), 0)),
        scratch_shapes=[pltpu.VMEM((GATHER_LAG + 1, tb, dw), jnp.uint32), pltpu.VMEM((d, de), BF16),
                        pltpu.VMEM((d, de), BF16), pltpu.VMEM((de, d), BF16),
                        pltpu.SemaphoreType.DMA((GATHER_LAG + 1,))],
    )
    return pl.pallas_call(
        _moe_kernel,
        grid_spec=grid_spec,
        out_shape=jax.ShapeDtypeStruct((n_blocks * tb, d // 2), jnp.uint32),
        compiler_params=_cparams(1),
        name=f"moe_experts_l{layer}",
    )(block_expert, nact, slot_tok.reshape(n_blocks, 1, tb), hp, wg, wu, wd)


def _combine_kernel(pos_ref, y_hbm, w_ref, x_ref, g2_ref, gout_ref, o_ref, ybuf, sem, *, final_norm):
    s = pl.program_id(0)
    nb = pl.num_programs(0) - 1
    tc = ybuf.shape[2]

    def gather_row(r, k):
        slot = s % 2
        pltpu.make_async_copy(y_hbm.at[pl.ds(pos_ref[0, 0, r * TOP_K + k], 1)], ybuf.at[slot, k, pl.ds(r, 1)],
                              sem.at[slot]).start()

    def gather_rows():
        for r in range(tc):
            for k in range(TOP_K):
                gather_row(r, k)

    def gather_rows_rolled():
        def body(r, c):
            for k in range(TOP_K):
                gather_row(r, k)
            return c
        lax.fori_loop(0, tc, body, 0, unroll=8)

    def finish(also_gather):
        slot = (s - 1) % 2
        for k in range(TOP_K):
            pltpu.make_async_copy(y_hbm.at[pl.ds(0, tc)], ybuf.at[slot, k], sem.at[slot]).wait()
        if also_gather:
            gather_rows()
        w = w_ref[...]
        moe = (_unpack_bf16_pairs(ybuf[slot, 0], F32) * w[:, 0:1]
               + _unpack_bf16_pairs(ybuf[slot, 1], F32) * w[:, 1:2])
        xn = x_ref[...] + g2_ref[0] * moe
        if final_norm:
            ms = jnp.mean(xn * xn, axis=-1, keepdims=True)
            xn = xn * lax.rsqrt(ms + RMS_EPS) * gout_ref[...]
        o_ref[...] = xn

    pl.when(s == 0)(gather_rows_rolled)
    pl.when(jnp.logical_and(s >= 1, s < nb))(lambda: finish(True))
    pl.when(s == nb)(lambda: finish(False))


def _combine_call(y, dest, w, xa, modr, gout, *, layer, n_ctx_rows, seq, latent_only, final_norm):
    t, d = xa.shape
    tc = 256
    assert n_ctx_rows % tc == 0 and seq % tc == 0
    ncb = n_ctx_rows // tc
    bpb = seq // tc
    off = ncb if latent_only else 0
    nblk = t // tc - off
    base = layer * COND_ROWS
    prev = lambda s: jnp.maximum(s - 1, 0)

    def mrow(s):
        blk = prev(s) + off
        seg = jnp.where(blk < ncb, COND_ROWS // 2, (blk - ncb) // bpb)
        return ((base + seg) * N_MOD + 5, 0, 0)

    out_off = 0 if final_norm else off
    out_rows = nblk * tc if final_norm else t
    return pl.pallas_call(
        functools.partial(_combine_kernel, final_norm=final_norm),
        grid=(nblk + 1,),
        in_specs=[
            pl.BlockSpec((1, 1, TOP_K * tc), lambda s: (jnp.minimum(s, nblk - 1), 0, 0), memory_space=pltpu.SMEM),
            pl.BlockSpec(memory_space=pl.ANY),
            pl.BlockSpec((tc, TOP_K), lambda s: (prev(s), 0)),
            pl.BlockSpec((tc, d), lambda s: (prev(s) + off, 0)),
            pl.BlockSpec((1, 1, d), mrow),
            pl.BlockSpec((1, d), lambda s: (0, 0)),
        ],
        out_specs=pl.BlockSpec((tc, d), lambda s: (prev(s) + out_off, 0)),
        out_shape=jax.ShapeDtypeStruct((out_rows, d), F32),
        scratch_shapes=[pltpu.VMEM((2, TOP_K, tc, y.shape[1]), y.dtype), pltpu.SemaphoreType.DMA((2,))],
        input_output_aliases={} if final_norm else {3: 0},
        compiler_params=_cparams(1),
        name=f"moe_combine_l{layer}",
    )(dest.reshape(nblk, 1, TOP_K * tc), y, w, xa, modr, gout.reshape(1, d))


def kernel(x, c, ctx, c_ctx, ada_w, ada_b, norm_mix_g, norm_ffn_g, norm_out_g, da_w_in, da_w_out,
           da_lam_q1, da_lam_k1, da_lam_q2, da_lam_k2, da_subln_g, gw_w_in, gw_w_out, gw_sinks,
           moe_w_group, moe_w_expert, moe_w_gate, moe_w_up, moe_w_down):
    batch, seq, d = x.shape
    n_ctx = ctx.shape[1]
    tc_rows = batch * n_ctx
    depth = ada_w.shape[0]
    assert batch < COND_ROWS // 2 + 1 and depth == DEPTH

    xa = jnp.concatenate([ctx.reshape(tc_rows, d), x.reshape(batch * seq, d)], axis=0)
    cond = jnp.zeros((COND_ROWS, d), F32).at[:batch].set(c).at[COND_ROWS // 2].set(c_ctx)
    mod = _ada_call(cond, ada_w, ada_b)
    modr = mod.reshape(depth * COND_ROWS * N_MOD, 1, d)

    da_dim = d // (2 * DA_HEADS)
    gw_dim = d // GW_HEADS
    tb = 256
    out = None
    for i in range(depth):
        last = i == depth - 1
        need_ctx = not last
        jm = i // N_MIXERS
        if i % N_MIXERS == 0:
            qkv = _qkv_call(xa, modr, norm_mix_g[i], da_w_in[jm].astype(BF16), layer=i,
                            n_ctx_rows=tc_rows, seq=seq, n_rope_cols=2 * d, n_q_cols=d,
                            qscale=da_dim ** -0.5 * LOG2E, rope_dim=da_dim)
            o = _da_attention(qkv, (da_lam_q1[jm], da_lam_k1[jm], da_lam_q2[jm], da_lam_k2[jm]),
                              da_subln_g[jm], layer=i, batch=batch, seq=seq, n_ctx=n_ctx, need_ctx=need_ctx)
            w_out = da_w_out[jm]
        else:
            kv_w = GW_KV_HEADS * gw_dim
            qkv = _qkv_call(xa, modr, norm_mix_g[i], gw_w_in[jm].astype(BF16), layer=i,
                            n_ctx_rows=tc_rows, seq=seq, n_rope_cols=d + kv_w, n_q_cols=d,
                            qscale=gw_dim ** -0.5 * LOG2E, rope_dim=gw_dim)
            o = _gw_attention(qkv, gw_sinks[jm], layer=i, batch=batch, seq=seq, n_ctx=n_ctx,
                              need_ctx=need_ctx, d=d)
            w_out = gw_w_out[jm]
        w_r = jnp.concatenate([moe_w_group[i], moe_w_expert[i]], axis=1)
        w_r = jnp.pad(w_r, ((0, 0), (0, ROUTER_LANES - w_r.shape[1])))
        r_hi = w_r.astype(BF16)
        r_lo = (w_r - r_hi.astype(F32)).astype(BF16)
        xa, h2, logits = _oproj_call(o, w_out.astype(BF16), xa, modr, norm_ffn_g[i], r_hi, r_lo, layer=i,
                                     n_ctx_rows=tc_rows, seq=seq, latent_only=last)
        w_tok, dest, slot_tok, block_expert, nact, n_blocks = _route(logits, tb)
        y = _moe_call(h2, slot_tok, block_expert, nact, moe_w_gate, moe_w_up, moe_w_down,
                      n_blocks=n_blocks, tb=tb, layer=i)
        res = _combine_call(y, dest, w_tok, xa, modr, norm_out_g, layer=i, n_ctx_rows=tc_rows, seq=seq,
                            latent_only=last, final_norm=last)
        if last:
            out = res
        else:
            xa = res
    return out.reshape(batch, seq, d)
```

```python
import functools
import math

import jax
import jax.numpy as jnp
from jax import lax
from jax.experimental import pallas as pl
from jax.experimental.pallas import tpu as pltpu

F32 = jnp.float32
BF16 = jnp.bfloat16

DEPTH = 4
N_MIXERS = 2
GRID_W = 64
DA_HEADS = 8
GW_HEADS = 32
GW_KV_HEADS = 4
GW_GROUP = GW_HEADS // GW_KV_HEADS
WINDOW = 128
N_GROUPS = 4
EXPERTS_PER_GROUP = 8
N_EXPERTS = N_GROUPS * EXPERTS_PER_GROUP
TOP_K = 2
ROPE_BASE = 10000.0
RMS_EPS = 1e-6
NEG_INF = -1e30
LOG2E = math.log2(math.e)

LANES = 128
COND_ROWS = 32
N_MOD = 6
ROUTER_LANES = 128
EPILOGUE_ROWS = 256
DA_KEY_BLOCK = 256
VMEM_LIMIT = 56 * 1024 * 1024


def _cparams(n_axes, vmem=VMEM_LIMIT):
    return pltpu.CompilerParams(dimension_semantics=("arbitrary",) * n_axes, vmem_limit_bytes=vmem)


def _dot(a, b):
    return jnp.dot(a, b, preferred_element_type=F32)


def _dot_nt(a, b):
    return lax.dot_general(a, b, (((1,), (1,)), ((), ())), preferred_element_type=F32)


def _ada_kernel(c_ref, w_ref, b_ref, o_ref):
    c = c_ref[...]
    s = (c / (1.0 + jnp.exp(-c))).astype(BF16)
    o_ref[...] = _dot(s, w_ref[...].astype(BF16)) + b_ref[...]


def _ada_call(cond, ada_w, ada_b):
    depth, d, n = ada_w.shape
    tn = 1024
    return pl.pallas_call(
        _ada_kernel,
        grid=(depth, n // tn),
        in_specs=[
            pl.BlockSpec((COND_ROWS, d), lambda l, j: (0, 0)),
            pl.BlockSpec((None, d, tn), lambda l, j: (l, 0, j)),
            pl.BlockSpec((None, 1, tn), lambda l, j: (l, 0, j)),
        ],
        out_specs=pl.BlockSpec((None, COND_ROWS, tn), lambda l, j: (l, 0, j)),
        out_shape=jax.ShapeDtypeStruct((depth, COND_ROWS, n), F32),
        compiler_params=_cparams(2),
        name="ada_mod",
    )(cond, ada_w, ada_b.reshape(depth, 1, n))


def _rope_tables(seq, dim):
    rows = seq // GRID_W
    row = jnp.repeat(jnp.arange(rows, dtype=F32), GRID_W)
    col = jnp.tile(jnp.arange(GRID_W, dtype=F32), rows)
    half = dim // 2
    inv = ROPE_BASE ** (-jnp.arange(0, half, 2, dtype=F32) / half)
    ang_r = row[:, None] * inv[None]
    ang_c = col[:, None] * inv[None]
    cr, sr, cc, sc = jnp.cos(ang_r), jnp.sin(ang_r), jnp.cos(ang_c), jnp.sin(ang_c)
    z = jnp.zeros_like(sr)
    reps = LANES // dim
    cos_t = jnp.tile(jnp.concatenate([cr, cr, cc, cc], axis=1), (1, reps))
    sin_a = jnp.tile(jnp.concatenate([-sr, z, -sc, z], axis=1), (1, reps))
    sin_b = jnp.tile(jnp.concatenate([z, sr, z, sc], axis=1), (1, reps))
    return cos_t, sin_a, sin_b


def _epilogue_tables(seq, dim, qscale, tm):
    cos_t, sin_a, sin_b = _rope_tables(seq, dim)
    one = jnp.ones((tm, LANES), F32)
    zero = jnp.zeros((tm, LANES), F32)
    cos_all = jnp.concatenate([cos_t * qscale, cos_t, one, one * qscale], axis=0)
    sa_all = jnp.concatenate([sin_a * qscale, sin_a, zero, zero], axis=0)
    sb_all = jnp.concatenate([sin_b * qscale, sin_b, zero, zero], axis=0)
    return cos_all, sa_all, sb_all


def _qkv_kernel(x_ref, sh_ref, sc_ref, g_ref, w_ref, cos_l, sa_l, sb_l, cos_r, sa_r, sb_r, o_ref, h_scr, *,
                rope_q, tn):
    n_tiles = tn // LANES
    tm = h_scr.shape[0]
    rows = min(tm, EPILOGUE_ROWS)

    def chunk(rc, first_col_block):
        rs = slice(rc * rows, (rc + 1) * rows)
        if first_col_block:
            x = x_ref[rs, :]
            ms = jnp.mean(x * x, axis=-1, keepdims=True)
            y = x * lax.rsqrt(ms + RMS_EPS) * g_ref[...]
            h = (y * (1.0 + sc_ref[0]) + sh_ref[0]).astype(BF16)
            h_scr[rs, :] = h
        else:
            h = h_scr[rs, :]
        acc = _dot(h, w_ref[...])
        for ci in range(n_tiles):
            cos_ref, sa_ref, sb_ref = (cos_l, sa_l, sb_l) if ci < n_tiles // 2 else (cos_r, sa_r, sb_r)
            a = acc[:, ci * LANES:(ci + 1) * LANES]
            a = (a * cos_ref[rs, :] + pltpu.roll(a, LANES - rope_q, 1) * sa_ref[rs, :]
                 + pltpu.roll(a, rope_q, 1) * sb_ref[rs, :])
            o_ref[rs, ci * LANES:(ci + 1) * LANES] = a.astype(BF16)

    def all_chunks(first_col_block):
        for rc in range(tm // rows):
            chunk(rc, first_col_block)

    j = pl.program_id(1)
    pl.when(j == 0)(lambda: all_chunks(True))
    pl.when(j != 0)(lambda: all_chunks(False))


def _qkv_call(xa, modr, gain, w, *, layer, n_ctx_rows, seq, n_rope_cols, n_q_cols, qscale, rope_dim):
    t, d = xa.shape
    n = w.shape[1]
    tn = 512
    hn = tn // 2
    tm = min(1024, seq, n_ctx_rows)
    assert t % tm == 0 and n_ctx_rows % tm == 0 and seq % tm == 0 and n % tn == 0
    assert n_q_cols % hn == 0 and n_rope_cols % hn == 0
    ncb = n_ctx_rows // tm
    bpb = seq // tm
    base = layer * COND_ROWS
    tables = _epilogue_tables(seq, rope_dim, qscale, tm)

    def seg(r):
        return jnp.where(r < ncb, COND_ROWS // 2, (r - ncb) // bpb)

    def tab_spec(half):
        def idx(r, j):
            col0 = j * tn + half * hn
            p = (r - ncb) % bpb
            lat = jnp.where(col0 < n_q_cols, p, jnp.where(col0 < n_rope_cols, bpb + p, 2 * bpb))
            ctx = jnp.where(col0 < n_q_cols, 2 * bpb + 1, 2 * bpb)
            return (jnp.where(r >= ncb, lat, ctx), 0)
        return pl.BlockSpec((tm, LANES), idx)

    return pl.pallas_call(
        functools.partial(_qkv_kernel, rope_q=rope_dim // 4, tn=tn),
        grid=(t // tm, n // tn),
        in_specs=[
            pl.BlockSpec((tm, d), lambda r, j: (r, 0)),
            pl.BlockSpec((1, 1, d), lambda r, j: ((base + seg(r)) * N_MOD + 0, 0, 0)),
            pl.BlockSpec((1, 1, d), lambda r, j: ((base + seg(r)) * N_MOD + 1, 0, 0)),
            pl.BlockSpec((1, d), lambda r, j: (0, 0)),
            pl.BlockSpec((d, tn), lambda r, j: (0, j)),
        ] + [tab_spec(0)] * 3 + [tab_spec(1)] * 3,
        out_specs=pl.BlockSpec((tm, tn), lambda r, j: (r, j)),
        out_shape=jax.ShapeDtypeStruct((t, n), BF16),
        scratch_shapes=[pltpu.VMEM((tm, d), BF16)],
        compiler_params=_cparams(2),
        name=f"qkv_proj_l{layer}",
    )(xa, modr, modr, gain.reshape(1, d), w, *tables, *tables)


def _da_body(q_ref, kc_ref, vc_ref, kx_ref, vx_ref, lq1, lk1, lq2, lk2, g_ref, o_ref, *, lam_init, has_x):
    dh = q_ref.shape[1] // 2
    lam = (jnp.exp(jnp.sum(lq1[...] * lk1[...], axis=-1, keepdims=True))
           - jnp.exp(jnp.sum(lq2[...] * lk2[...], axis=-1, keepdims=True)) + lam_init)

    tq = q_ref.shape[0]
    kb = DA_KEY_BLOCK
    key_blocks = [(kc_ref, vc_ref, i * kb) for i in range(kc_ref.shape[0] // kb)]
    if has_x:
        key_blocks += [(kx_ref, vx_ref, i * kb) for i in range(kx_ref.shape[0] // kb)]

    def component(lo):
        qc = q_ref[:, lo:lo + dh]
        m = jnp.full((tq, 1), NEG_INF, F32)
        den = jnp.zeros((tq, 1), F32)
        acc = jnp.zeros((tq, vc_ref.shape[1]), F32)
        for k_ref, v_ref, off in key_blocks:
            s = _dot_nt(qc, k_ref[off:off + kb, lo:lo + dh])
            m_new = jnp.maximum(m, jnp.max(s, axis=-1, keepdims=True))
            alpha = jnp.exp2(m - m_new)
            e = jnp.exp2(s - m_new)
            den = den * alpha + jnp.sum(e, axis=-1, keepdims=True)
            acc = acc * alpha + _dot(e.astype(BF16), v_ref[off:off + kb, :])
            m = m_new
        return acc, 1.0 / den

    a1, r1 = component(0)
    a2, r2 = component(dh)
    o = a1 * r1 - a2 * (lam * r2)
    ms = jnp.mean(o * o, axis=-1, keepdims=True)
    o = o * lax.rsqrt(ms + RMS_EPS) * g_ref[...] * (1.0 - lam_init)
    o_ref[...] = o.astype(BF16)


def _da_kernel(*refs, lam_init, nqb, has_ctx_steps):
    body = functools.partial(_da_body, *refs, lam_init=lam_init)
    if has_ctx_steps:
        qb = pl.program_id(2)
        pl.when(qb < nqb)(lambda: body(has_x=True))
        pl.when(qb >= nqb)(lambda: body(has_x=False))
    else:
        body(has_x=True)


def _da_attention(qkv, lam_vecs, subln_g, *, layer, batch, seq, n_ctx, need_ctx):
    t, n3 = qkv.shape
    d = n3 // 3
    hw = d // DA_HEADS
    tc_rows = batch * n_ctx
    tq = min(256, seq)
    assert seq % tq == 0 and n_ctx % tq == 0 and tc_rows % seq == 0 and hw % LANES == 0
    lam_init = 0.8 - 0.6 * math.exp(-0.3 * layer)
    nh = DA_HEADS
    small = [v.reshape(1, -1) for v in lam_vecs] + [subln_g.reshape(1, hw)]
    small_specs = [pl.BlockSpec(v.shape, lambda b, h, qb: (0, 0)) for v in small]
    qoff = tc_rows // tq
    nqb = seq // tq
    ncq = n_ctx // tq if need_ctx else 0
    xoff = tc_rows // seq

    def q_row(b, qb):
        lat = qoff + b * nqb + jnp.minimum(qb, nqb - 1)
        return jnp.where(qb < nqb, lat, b * ncq + (qb - nqb)) if need_ctx else lat

    def o_row(b, qb):
        return q_row(b, qb) if need_ctx else b * nqb + qb

    return pl.pallas_call(
        functools.partial(_da_kernel, lam_init=lam_init, nqb=nqb, has_ctx_steps=need_ctx),
        grid=(batch, nh, nqb + ncq),
        in_specs=[
            pl.BlockSpec((tq, hw), lambda b, h, qb: (q_row(b, qb), h)),
            pl.BlockSpec((n_ctx, hw), lambda b, h, qb: (b, nh + h)),
            pl.BlockSpec((n_ctx, hw), lambda b, h, qb: (b, 2 * nh + h)),
            pl.BlockSpec((seq, hw), lambda b, h, qb: (xoff + b, nh + h)),
            pl.BlockSpec((seq, hw), lambda b, h, qb: (xoff + b, 2 * nh + h)),
        ] + small_specs,
        out_specs=pl.BlockSpec((tq, hw), lambda b, h, qb: (o_row(b, qb), h)),
        out_shape=jax.ShapeDtypeStruct((t if need_ctx else batch * seq, d), BF16),
        compiler_params=_cparams(3),
        name=f"diff_attn_l{layer}",
    )(qkv, qkv, qkv, qkv, qkv, *small)


def _gw_body(sink_ref, q_ref, kc_ref, vc_ref, kx_ref, vx_ref, o_ref, *, has_x, heads_per_step, group, dim, nqb):
    gp = pl.program_id(1)
    tq = q_ref.shape[0]
    half = LANES // 2
    assert dim == half
    tiles = group * dim // LANES
    kv_heads = heads_per_step // group
    if has_x:
        qb = pl.program_id(2)
        st_p = pl.multiple_of(jnp.maximum(qb - 1, 0) * tq, tq)
        st_m = pl.multiple_of(qb * tq, tq)
        st_n = pl.multiple_of(jnp.minimum(qb + 1, nqb - 1) * tq, tq)
        k_all = jnp.concatenate([kc_ref[...]] + [kx_ref[pl.ds(s, tq), :] for s in (st_p, st_m, st_n)], axis=0)
        v_all = jnp.concatenate([vc_ref[...]] + [vx_ref[pl.ds(s, tq), :] for s in (st_p, st_m, st_n)], axis=0)
        n_ctx = kc_ref.shape[0]
        r = lax.broadcasted_iota(jnp.int32, (tq, tq), 0)
        c = lax.broadcasted_iota(jnp.int32, (tq, tq), 1)
        zero = jnp.zeros((tq, tq), F32)
        bias_p = jnp.where(jnp.logical_and(c >= r, qb >= 1), 0.0, NEG_INF)
        bias_n = jnp.where(jnp.logical_and(c <= r, qb + 1 < nqb), 0.0, NEG_INF)
        bias = jnp.concatenate([jnp.zeros((tq, n_ctx), F32), bias_p, zero, bias_n], axis=1)
        bias = jnp.concatenate([bias] * tiles, axis=0)
    else:
        k_all = kc_ref[...]
        v_all = vc_ref[...]
        bias = None
    lo = lax.broadcasted_iota(jnp.int32, k_all.shape, 1) < half

    def half_variants(x):
        xf = x.astype(F32)
        xs = pltpu.roll(xf, half, 1)
        keep_lo = lambda a: jnp.where(lo, a, 0.0).astype(BF16)
        keep_hi = lambda a: jnp.where(lo, 0.0, a).astype(BF16)
        return [(keep_lo(xf), keep_hi(xs)), (keep_lo(xs), keep_hi(xf))]

    k_var = half_variants(k_all)
    v_var = half_variants(v_all)
    for gi in range(kv_heads):
        kz = k_var[gi]
        vz = v_var[gi]
        q_stack = jnp.concatenate(
            [q_ref[:, (gi * tiles + j) * LANES:(gi * tiles + j + 1) * LANES] for j in range(tiles)], axis=0)
        out = None
        for par in range(2):
            s = _dot_nt(q_stack, kz[par])
            if bias is not None:
                s = s + bias
            head0 = gp * heads_per_step + gi * group + par
            sk = jnp.concatenate(
                [jnp.full((tq, 1), sink_ref[head0 + 2 * j] * LOG2E, F32) for j in range(tiles)], axis=0)
            m = jnp.maximum(jnp.max(s, axis=-1, keepdims=True), sk)
            e = jnp.exp2(s - m)
            den = jnp.sum(e, axis=-1, keepdims=True) + jnp.exp2(sk - m)
            o = _dot(e.astype(BF16), vz[par]) * (1.0 / den)
            out = o if out is None else out + o
        for j in range(tiles):
            o_ref[:, (gi * tiles + j) * LANES:(gi * tiles + j + 1) * LANES] = out[j * tq:(j + 1) * tq].astype(BF16)


def _gw_kernel(*refs, nqb, has_ctx_steps, **kw):
    body = functools.partial(_gw_body, *refs, nqb=nqb, **kw)
    if has_ctx_steps:
        qb = pl.program_id(2)
        pl.when(qb < nqb)(lambda: body(has_x=True))
        pl.when(qb >= nqb)(lambda: body(has_x=False))
    else:
        body(has_x=True)


def _gw_attention(qkv, sinks, *, layer, batch, seq, n_ctx, need_ctx, d):
    t = qkv.shape[0]
    dim = d // GW_HEADS
    kv_per_step = LANES // dim
    hps = kv_per_step * GW_GROUP
    qw = hps * dim
    n_gp = GW_KV_HEADS // kv_per_step
    tq = WINDOW
    tc_rows = batch * n_ctx
    assert seq % tq == 0 and n_ctx % tq == 0 and tc_rows % seq == 0
    nqb = seq // tq
    ncq = n_ctx // tq if need_ctx else 0
    qoff = tc_rows // tq
    xoff = tc_rows // seq
    kcol = d // LANES
    vcol = kcol + GW_KV_HEADS * dim // LANES

    def q_row(b, qb):
        lat = qoff + b * nqb + jnp.minimum(qb, nqb - 1)
        return jnp.where(qb < nqb, lat, b * ncq + (qb - nqb)) if need_ctx else lat

    def o_row(b, qb):
        return q_row(b, qb) if need_ctx else b * nqb + qb

    return pl.pallas_call(
        functools.partial(_gw_kernel, nqb=nqb, has_ctx_steps=need_ctx, heads_per_step=hps, group=GW_GROUP, dim=dim),
        grid=(batch, n_gp, nqb + ncq),
        in_specs=[
            pl.BlockSpec(memory_space=pltpu.SMEM),
            pl.BlockSpec((tq, qw), lambda b, g, qb: (q_row(b, qb), g)),
            pl.BlockSpec((n_ctx, LANES), lambda b, g, qb: (b, kcol + g)),
            pl.BlockSpec((n_ctx, LANES), lambda b, g, qb: (b, vcol + g)),
            pl.BlockSpec((seq, LANES), lambda b, g, qb: (xoff + b, kcol + g)),
            pl.BlockSpec((seq, LANES), lambda b, g, qb: (xoff + b, vcol + g)),
        ],
        out_specs=pl.BlockSpec((tq, qw), lambda b, g, qb: (o_row(b, qb), g)),
        out_shape=jax.ShapeDtypeStruct((t if need_ctx else batch * seq, d), BF16),
        compiler_params=_cparams(3),
        name=f"gqa_attn_l{layer}",
    )(sinks, qkv, qkv, qkv, qkv, qkv)


def _pack_bf16_pairs(h):
    half = h.shape[1] // 2
    hi = lax.bitcast_convert_type(h[:, :half].astype(BF16).astype(F32), jnp.uint32)
    lo = lax.bitcast_convert_type(h[:, half:].astype(BF16).astype(F32), jnp.uint32)
    return hi | (lo >> 16)


def _unpack_bf16_pairs(u, dtype=BF16):
    hi = lax.bitcast_convert_type(u & jnp.uint32(0xFFFF0000), F32)
    lo = lax.bitcast_convert_type(u << 16, F32)
    return jnp.concatenate([hi, lo], axis=1).astype(dtype)


def _oproj_kernel(o_ref, w_ref, x_ref, g1_ref, sh_ref, sc_ref, gn_ref, rcat_ref, rhi_ref,
                  xo_ref, h_ref, lg_ref):
    tm = o_ref.shape[0]
    rows = min(tm, EPILOGUE_ROWS)
    for rc in range(tm // rows):
        rs = slice(rc * rows, (rc + 1) * rows)
        xn = x_ref[rs, :] + g1_ref[0] * _dot(o_ref[rs, :], w_ref[...])
        xo_ref[rs, :] = xn
        ms = jnp.mean(xn * xn, axis=-1, keepdims=True)
        h = xn * lax.rsqrt(ms + RMS_EPS) * gn_ref[...]
        h = h * (1.0 + sc_ref[0]) + sh_ref[0]
        h_ref[rs, :] = _pack_bf16_pairs(h)
        h_hi = h.astype(BF16)
        h_lo = (h - h_hi.astype(F32)).astype(BF16)
        hw = _dot(h_hi, rcat_ref[...])
        lg = hw[:, :ROUTER_LANES] + (hw[:, ROUTER_LANES:] + _dot(h_lo, rhi_ref[...]))
        lg_ref[rs, :] = _route_rows(lg)


def _route_rows(lg):
    lane = lax.broadcasted_iota(jnp.int32, lg.shape, 1).astype(F32)
    far = jnp.float32(2 * ROUTER_LANES)

    def first_lane(cond):
        return jnp.min(jnp.where(cond, lane, far), axis=-1, keepdims=True)

    is_g = lane < N_GROUPS
    gl = jnp.where(is_g, lg, NEG_INF)
    gmax = jnp.max(gl, axis=-1, keepdims=True)
    g_sel = first_lane(jnp.logical_and(is_g, gl == gmax))
    g_w = 1.0 / jnp.sum(jnp.where(is_g, jnp.exp(gl - gmax), 0.0), axis=-1, keepdims=True)
    lo_e = N_GROUPS + g_sel * EXPERTS_PER_GROUP
    in_grp = jnp.logical_and(lane >= lo_e, lane < lo_e + EXPERTS_PER_GROUP)
    el = jnp.where(in_grp, lg, NEG_INF)
    ee = jnp.where(in_grp, jnp.exp(el - jnp.max(el, axis=-1, keepdims=True)), 0.0)
    prob = ee / jnp.sum(ee, axis=-1, keepdims=True)
    p1 = jnp.max(jnp.where(in_grp, prob, -1.0), axis=-1, keepdims=True)
    i1 = first_lane(jnp.logical_and(in_grp, prob == p1))
    rest = jnp.where(jnp.logical_and(in_grp, lane != i1), prob, -1.0)
    p2 = jnp.max(rest, axis=-1, keepdims=True)
    i2 = first_lane(jnp.logical_and(rest >= 0.0, rest == p2))
    den = p1 + p2
    vals = (g_w * (p1 / den), g_w * (p2 / den), i1 - N_GROUPS, i2 - N_GROUPS)
    out = jnp.zeros_like(lg)
    for k, v in enumerate(vals):
        out = jnp.where(lane == k, v, out)
    return out


def _oproj_call(o, w, xa, modr, gain, r_hi, r_lo, *, layer, n_ctx_rows, seq, latent_only):
    t, d = xa.shape
    tm = 512
    assert n_ctx_rows % tm == 0 and seq % tm == 0
    ncb = n_ctx_rows // tm
    bpb = seq // tm
    off = ncb if latent_only else 0
    nblk = t // tm - off
    base = layer * COND_ROWS

    def mrow(r, chunk):
        blk = r + off
        seg = jnp.where(blk < ncb, COND_ROWS // 2, (blk - ncb) // bpb)
        return ((base + seg) * N_MOD + chunk, 0, 0)

    const = lambda r: (0, 0)
    return pl.pallas_call(
        _oproj_kernel,
        grid=(nblk,),
        in_specs=[
            pl.BlockSpec((tm, d), lambda r: (r, 0)),
            pl.BlockSpec((d, d), const, pipeline_mode=pl.Buffered(1)),
            pl.BlockSpec((tm, d), lambda r: (r + off, 0)),
            pl.BlockSpec((1, 1, d), lambda r: mrow(r, 2)),
            pl.BlockSpec((1, 1, d), lambda r: mrow(r, 3)),
            pl.BlockSpec((1, 1, d), lambda r: mrow(r, 4)),
            pl.BlockSpec((1, d), const),
            pl.BlockSpec((d, 2 * ROUTER_LANES), const),
            pl.BlockSpec((d, ROUTER_LANES), const),
        ],
        out_specs=[
            pl.BlockSpec((tm, d), lambda r: (r + off, 0)),
            pl.BlockSpec((tm, d // 2), lambda r: (r, 0)),
            pl.BlockSpec((tm, ROUTER_LANES), lambda r: (r, 0)),
        ],
        out_shape=[
            jax.ShapeDtypeStruct((t, d), F32),
            jax.ShapeDtypeStruct((nblk * tm, d // 2), jnp.uint32),
            jax.ShapeDtypeStruct((nblk * tm, ROUTER_LANES), F32),
        ],
        input_output_aliases={2: 0},
        compiler_params=_cparams(1),
        name=f"out_proj_l{layer}",
    )(o, w, xa, modr, modr, modr, gain.reshape(1, d), jnp.concatenate([r_hi, r_lo], axis=1), r_hi)


def _inclusive_cumsum_rows(m):
    n, k = m.shape
    blk = LANES
    assert n % blk == 0
    tri = (jnp.arange(blk)[:, None] >= jnp.arange(blk)[None, :]).astype(F32)
    m3 = m.reshape(n // blk, blk, k)
    within = jnp.einsum("ij,bjk->bik", tri, m3, precision=lax.Precision.HIGHEST)
    sums = within[:, -1, :]
    nb = n // blk
    tri_b = (jnp.arange(nb)[:, None] > jnp.arange(nb)[None, :]).astype(F32)
    offs = jnp.dot(tri_b, sums, precision=lax.Precision.HIGHEST)
    return (within + offs[:, None, :]).reshape(n, k)


def _route(routed, tb):
    t = routed.shape[0]
    w = routed[:, :TOP_K]
    eid = routed[:, TOP_K:2 * TOP_K].astype(jnp.int32).reshape(-1)
    a = t * TOP_K
    onehot = (eid[:, None] == jnp.arange(N_EXPERTS, dtype=jnp.int32)[None]).astype(F32)
    csum = _inclusive_cumsum_rows(onehot)
    counts = csum[-1].astype(jnp.int32)
    nblk = (counts + tb - 1) // tb
    bend = jnp.cumsum(nblk)
    bstart = bend - nblk
    slot = jnp.sum(onehot * (csum - 1.0 + (bstart * tb).astype(F32)[None]), axis=-1)
    dest = slot.astype(jnp.int32)
    n_blocks = -(-a // tb) + N_EXPERTS
    blocks = jnp.arange(n_blocks, dtype=jnp.int32)
    block_expert = jnp.minimum(jnp.sum((bend[None, :] <= blocks[:, None]).astype(jnp.int32), axis=1), N_EXPERTS - 1)
    nact = bend[-1:].astype(jnp.int32)
    sorted_tok = (jnp.argsort(eid, stable=True) // TOP_K).astype(jnp.int32)
    start = jnp.cumsum(counts) - counts
    slot_e = jnp.repeat(block_expert, tb)
    off = jnp.arange(n_blocks * tb, dtype=jnp.int32) - bstart[slot_e] * tb
    src = jnp.clip(start[slot_e] + off, 0, a - 1)
    slot_tok = jnp.where(off < counts[slot_e], sorted_tok[src], 0)
    return w, dest, slot_tok, block_expert, nact, n_blocks


GATHER_LAG = 2


def _moe_kernel(be_ref, nact_ref, st_ref, h_hbm, wg_ref, wu_ref, wd_ref, y_ref, xbuf, wg_s, wu_s, wd_s, sem):
    s = pl.program_id(0)
    nact = nact_ref[0]
    nbuf, tb = xbuf.shape[0], xbuf.shape[1]
    blk = s - GATHER_LAG
    do_gather = s < nact
    do_ffn = jnp.logical_and(blk >= 0, blk < nact)
    prev_blk = jnp.maximum(blk - 1, 0)
    fresh = jnp.logical_or(blk == 0, be_ref[jnp.maximum(blk, 0)] != be_ref[prev_blk])

    def gather_row(r, slot):
        pltpu.make_async_copy(h_hbm.at[pl.ds(st_ref[0, 0, r], 1)], xbuf.at[slot, pl.ds(r, 1)], sem.at[slot]).start()

    def gather_rows():
        for r in range(tb):
            gather_row(r, s % nbuf)

    def gather_rows_rolled():
        lax.fori_loop(0, tb, lambda r, c: (gather_row(r, s % nbuf), c)[1], 0, unroll=8)

    def ffn(also_gather):
        slot = blk % nbuf
        pltpu.make_async_copy(h_hbm.at[pl.ds(0, tb)], xbuf.at[slot], sem.at[slot]).wait()
        xb = _unpack_bf16_pairs(xbuf[slot])
        if also_gather:
            gather_rows()
        g = _dot(xb, wg_s[...])
        u = _dot(xb, wu_s[...])
        act = (g / (1.0 + jnp.exp(-g))) * u
        y_ref[...] = _pack_bf16_pairs(_dot(act.astype(BF16), wd_s[...]))

    @pl.when(jnp.logical_and(do_ffn, fresh))
    def _():
        wg_s[...] = wg_ref[...].astype(BF16)
        wu_s[...] = wu_ref[...].astype(BF16)
        wd_s[...] = wd_ref[...].astype(BF16)

    pl.when(jnp.logical_and(do_ffn, do_gather))(lambda: ffn(True))
    pl.when(jnp.logical_and(do_ffn, jnp.logical_not(do_gather)))(lambda: ffn(False))
    pl.when(jnp.logical_and(jnp.logical_not(do_ffn), do_gather))(gather_rows_rolled)

    @pl.when(jnp.logical_and(blk >= 0, blk >= nact))
    def _():
        y_ref[...] = jnp.zeros_like(y_ref)


def _moe_call(hp, slot_tok, block_expert, nact, wg, wu, wd, *, n_blocks, tb, layer):
    dw = hp.shape[1]
    d, de = wg.shape[2], wg.shape[3]
    prev = lambda s: jnp.maximum(s - GATHER_LAG, 0)
    grid_spec = pltpu.PrefetchScalarGridSpec(
        num_scalar_prefetch=2,
        grid=(n_blocks + GATHER_LAG,),
        in_specs=[
            pl.BlockSpec((1, 1, tb), lambda s, be, na: (jnp.minimum(s, n_blocks - 1), 0, 0),
                         memory_space=pltpu.SMEM),
            pl.BlockSpec(memory_space=pl.ANY),
            pl.BlockSpec((None, None, d, de), lambda s, be, na: (layer, be[prev(s)], 0, 0)),
            pl.BlockSpec((None, None, d, de), lambda s, be, na: (layer, be[prev(s)], 0, 0)),
            pl.BlockSpec((None, None, de, d), lambda s, be, na: (layer, be[prev(s)], 0, 0)),
        ],
        out_specs=pl.BlockSpec((tb, d // 2), lambda s, be, na: (prev(s), 0)),
        scratch_shapes=[pltpu.VMEM((GATHER_LAG + 1, tb, dw), jnp.uint32), pltpu.VMEM((d, de), BF16),
                        pltpu.VMEM((d, de), BF16), pltpu.VMEM((de, d), BF16),
                        pltpu.SemaphoreType.DMA((GATHER_LAG + 1,))],
    )
    return pl.pallas_call(
        _moe_kernel,
        grid_spec=grid_spec,
        out_shape=jax.ShapeDtypeStruct((n_blocks * tb, d // 2), jnp.uint32),
        compiler_params=_cparams(1),
        name=f"moe_experts_l{layer}",
    )(block_expert, nact, slot_tok.reshape(n_blocks, 1, tb), hp, wg, wu, wd)


def _combine_kernel(pos_ref, y_hbm, w_ref, x_ref, g2_ref, gout_ref, o_ref, ybuf, sem, *, final_norm):
    s = pl.program_id(0)
    nb = pl.num_programs(0) - 1
    tc = ybuf.shape[2]

    def gather_row(r, k):
        slot = s % 2
        pltpu.make_async_copy(y_hbm.at[pl.ds(pos_ref[0, 0, r * TOP_K + k], 1)], ybuf.at[slot, k, pl.ds(r, 1)],
                              sem.at[slot]).start()

    def gather_rows():
        for r in range(tc):
            for k in range(TOP_K):
                gather_row(r, k)

    def gather_rows_rolled():
        def body(r, c):
            for k in range(TOP_K):
                gather_row(r, k)
            return c
        lax.fori_loop(0, tc, body, 0, unroll=8)

    def finish(also_gather):
        slot = (s - 1) % 2
        for k in range(TOP_K):
            pltpu.make_async_copy(y_hbm.at[pl.ds(0, tc)], ybuf.at[slot, k], sem.at[slot]).wait()
        if also_gather:
            gather_rows()
        w = w_ref[...]
        moe = (_unpack_bf16_pairs(ybuf[slot, 0], F32) * w[:, 0:1]
               + _unpack_bf16_pairs(ybuf[slot, 1], F32) * w[:, 1:2])
        xn = x_ref[...] + g2_ref[0] * moe
        if final_norm:
            ms = jnp.mean(xn * xn, axis=-1, keepdims=True)
            xn = xn * lax.rsqrt(ms + RMS_EPS) * gout_ref[...]
        o_ref[...] = xn

    pl.when(s == 0)(gather_rows_rolled)
    pl.when(jnp.logical_and(s >= 1, s < nb))(lambda: finish(True))
    pl.when(s == nb)(lambda: finish(False))


def _combine_call(y, dest, w, xa, modr, gout, *, layer, n_ctx_rows, seq, latent_only, final_norm):
    t, d = xa.shape
    tc = 256
    assert n_ctx_rows % tc == 0 and seq % tc == 0
    ncb = n_ctx_rows // tc
    bpb = seq // tc
    off = ncb if latent_only else 0
    nblk = t // tc - off
    base = layer * COND_ROWS
    prev = lambda s: jnp.maximum(s - 1, 0)

    def mrow(s):
        blk = prev(s) + off
        seg = jnp.where(blk < ncb, COND_ROWS // 2, (blk - ncb) // bpb)
        return ((base + seg) * N_MOD + 5, 0, 0)

    out_off = 0 if final_norm else off
    out_rows = nblk * tc if final_norm else t
    return pl.pallas_call(
        functools.partial(_combine_kernel, final_norm=final_norm),
        grid=(nblk + 1,),
        in_specs=[
            pl.BlockSpec((1, 1, TOP_K * tc), lambda s: (jnp.minimum(s, nblk - 1), 0, 0), memory_space=pltpu.SMEM),
            pl.BlockSpec(memory_space=pl.ANY),
            pl.BlockSpec((tc, TOP_K), lambda s: (prev(s), 0)),
            pl.BlockSpec((tc, d), lambda s: (prev(s) + off, 0)),
            pl.BlockSpec((1, 1, d), mrow),
            pl.BlockSpec((1, d), lambda s: (0, 0)),
        ],
        out_specs=pl.BlockSpec((tc, d), lambda s: (prev(s) + out_off, 0)),
        out_shape=jax.ShapeDtypeStruct((out_rows, d), F32),
        scratch_shapes=[pltpu.VMEM((2, TOP_K, tc, y.shape[1]), y.dtype), pltpu.SemaphoreType.DMA((2,))],
        input_output_aliases={} if final_norm else {3: 0},
        compiler_params=_cparams(1),
        name=f"moe_combine_l{layer}",
    )(dest.reshape(nblk, 1, TOP_K * tc), y, w, xa, modr, gout.reshape(1, d))


def kernel(x, c, ctx, c_ctx, ada_w, ada_b, norm_mix_g, norm_ffn_g, norm_out_g, da_w_in, da_w_out,
           da_lam_q1, da_lam_k1, da_lam_q2, da_lam_k2, da_subln_g, gw_w_in, gw_w_out, gw_sinks,
           moe_w_group, moe_w_expert, moe_w_gate, moe_w_up, moe_w_down):
    batch, seq, d = x.shape
    n_ctx = ctx.shape[1]
    tc_rows = batch * n_ctx
    depth = ada_w.shape[0]
    assert batch < COND_ROWS // 2 + 1 and depth == DEPTH

    xa = jnp.concatenate([ctx.reshape(tc_rows, d), x.reshape(batch * seq, d)], axis=0)
    cond = jnp.zeros((COND_ROWS, d), F32).at[:batch].set(c).at[COND_ROWS // 2].set(c_ctx)
    mod = _ada_call(cond, ada_w, ada_b)
    modr = mod.reshape(depth * COND_ROWS * N_MOD, 1, d)

    da_dim = d // (2 * DA_HEADS)
    gw_dim = d // GW_HEADS
    tb = 256
    out = None
    for i in range(depth):
        last = i == depth - 1
        need_ctx = not last
        jm = i // N_MIXERS
        if i % N_MIXERS == 0:
            qkv = _qkv_call(xa, modr, norm_mix_g[i], da_w_in[jm].astype(BF16), layer=i,
                            n_ctx_rows=tc_rows, seq=seq, n_rope_cols=2 * d, n_q_cols=d,
                            qscale=da_dim ** -0.5 * LOG2E, rope_dim=da_dim)
            o = _da_attention(qkv, (da_lam_q1[jm], da_lam_k1[jm], da_lam_q2[jm], da_lam_k2[jm]),
                              da_subln_g[jm], layer=i, batch=batch, seq=seq, n_ctx=n_ctx, need_ctx=need_ctx)
            w_out = da_w_out[jm]
        else:
            kv_w = GW_KV_HEADS * gw_dim
            qkv = _qkv_call(xa, modr, norm_mix_g[i], gw_w_in[jm].astype(BF16), layer=i,
                            n_ctx_rows=tc_rows, seq=seq, n_rope_cols=d + kv_w, n_q_cols=d,
                            qscale=gw_dim ** -0.5 * LOG2E, rope_dim=gw_dim)
            o = _gw_attention(qkv, gw_sinks[jm], layer=i, batch=batch, seq=seq, n_ctx=n_ctx,
                              need_ctx=need_ctx, d=d)
            w_out = gw_w_out[jm]
        w_r = jnp.concatenate([moe_w_group[i], moe_w_expert[i]], axis=1)
        w_r = jnp.pad(w_r, ((0, 0), (0, ROUTER_LANES - w_r.shape[1])))
        r_hi = w_r.astype(BF16)
        r_lo = (w_r - r_hi.astype(F32)).astype(BF16)
        xa, h2, logits = _oproj_call(o, w_out.astype(BF16), xa, modr, norm_ffn_g[i], r_hi, r_lo, layer=i,
                                     n_ctx_rows=tc_rows, seq=seq, latent_only=last)
        w_tok, dest, slot_tok, block_expert, nact, n_blocks = _route(logits, tb)
        y = _moe_call(h2, slot_tok, block_expert, nact, moe_w_gate, moe_w_up, moe_w_down,
                      n_blocks=n_blocks, tb=tb, layer=i)
        res = _combine_call(y, dest, w_tok, xa, modr, norm_out_g, layer=i, n_ctx_rows=tc_rows, seq=seq,
                            latent_only=last, final_norm=last)
        if last:
            out = res
        else:
            xa = res
    return out.reshape(batch, seq, d)
```

```python
import functools
import math

import jax
import jax.numpy as jnp
from jax import lax
from jax.experimental import pallas as pl
from jax.experimental.pallas import tpu as pltpu

F32 = jnp.float32
BF16 = jnp.bfloat16

DEPTH = 4
N_MIXERS = 2
GRID_W = 64
DA_HEADS = 8
GW_HEADS = 32
GW_KV_HEADS = 4
GW_GROUP = GW_HEADS // GW_KV_HEADS
WINDOW = 128
N_GROUPS = 4
EXPERTS_PER_GROUP = 8
N_EXPERTS = N_GROUPS * EXPERTS_PER_GROUP
TOP_K = 2
ROPE_BASE = 10000.0
RMS_EPS = 1e-6
NEG_INF = -1e30
LOG2E = math.log2(math.e)

LANES = 128
COND_ROWS = 32
N_MOD = 6
ROUTER_LANES = 128
EPILOGUE_ROWS = 256
VMEM_LIMIT = 56 * 1024 * 1024


def _cparams(n_axes, vmem=VMEM_LIMIT):
    return pltpu.CompilerParams(dimension_semantics=("arbitrary",) * n_axes, vmem_limit_bytes=vmem)


def _dot(a, b):
    return jnp.dot(a, b, preferred_element_type=F32)


def _dot_nt(a, b):
    return lax.dot_general(a, b, (((1,), (1,)), ((), ())), preferred_element_type=F32)


def _ada_kernel(c_ref, w_ref, b_ref, o_ref):
    c = c_ref[...]
    s = (c / (1.0 + jnp.exp(-c))).astype(BF16)
    o_ref[...] = _dot(s, w_ref[...].astype(BF16)) + b_ref[...]


def _ada_call(cond, ada_w, ada_b):
    depth, d, n = ada_w.shape
    tn = 1024
    return pl.pallas_call(
        _ada_kernel,
        grid=(depth, n // tn),
        in_specs=[
            pl.BlockSpec((COND_ROWS, d), lambda l, j: (0, 0)),
            pl.BlockSpec((None, d, tn), lambda l, j: (l, 0, j)),
            pl.BlockSpec((None, 1, tn), lambda l, j: (l, 0, j)),
        ],
        out_specs=pl.BlockSpec((None, COND_ROWS, tn), lambda l, j: (l, 0, j)),
        out_shape=jax.ShapeDtypeStruct((depth, COND_ROWS, n), F32),
        compiler_params=_cparams(2),
        name="ada_mod",
    )(cond, ada_w, ada_b.reshape(depth, 1, n))


def _rope_tables(seq, dim):
    rows = seq // GRID_W
    row = jnp.repeat(jnp.arange(rows, dtype=F32), GRID_W)
    col = jnp.tile(jnp.arange(GRID_W, dtype=F32), rows)
    half = dim // 2
    inv = ROPE_BASE ** (-jnp.arange(0, half, 2, dtype=F32) / half)
    ang_r = row[:, None] * inv[None]
    ang_c = col[:, None] * inv[None]
    cr, sr, cc, sc = jnp.cos(ang_r), jnp.sin(ang_r), jnp.cos(ang_c), jnp.sin(ang_c)
    z = jnp.zeros_like(sr)
    reps = LANES // dim
    cos_t = jnp.tile(jnp.concatenate([cr, cr, cc, cc], axis=1), (1, reps))
    sin_a = jnp.tile(jnp.concatenate([-sr, z, -sc, z], axis=1), (1, reps))
    sin_b = jnp.tile(jnp.concatenate([z, sr, z, sc], axis=1), (1, reps))
    return cos_t, sin_a, sin_b


def _epilogue_tables(seq, dim, qscale, tm):
    cos_t, sin_a, sin_b = _rope_tables(seq, dim)
    one = jnp.ones((tm, LANES), F32)
    zero = jnp.zeros((tm, LANES), F32)
    cos_all = jnp.concatenate([cos_t * qscale, cos_t, one, one * qscale], axis=0)
    sa_all = jnp.concatenate([sin_a * qscale, sin_a, zero, zero], axis=0)
    sb_all = jnp.concatenate([sin_b * qscale, sin_b, zero, zero], axis=0)
    return cos_all, sa_all, sb_all


def _qkv_kernel(x_ref, sh_ref, sc_ref, g_ref, w_ref, cos_l, sa_l, sb_l, cos_r, sa_r, sb_r, o_ref, h_scr, *,
                rope_q, tn):
    n_tiles = tn // LANES
    tm = h_scr.shape[0]
    rows = min(tm, EPILOGUE_ROWS)

    def chunk(rc, first_col_block):
        rs = slice(rc * rows, (rc + 1) * rows)
        if first_col_block:
            x = x_ref[rs, :]
            ms = jnp.mean(x * x, axis=-1, keepdims=True)
            y = x * lax.rsqrt(ms + RMS_EPS) * g_ref[...]
            h = (y * (1.0 + sc_ref[0]) + sh_ref[0]).astype(BF16)
            h_scr[rs, :] = h
        else:
            h = h_scr[rs, :]
        acc = _dot(h, w_ref[...])
        for ci in range(n_tiles):
            cos_ref, sa_ref, sb_ref = (cos_l, sa_l, sb_l) if ci < n_tiles // 2 else (cos_r, sa_r, sb_r)
            a = acc[:, ci * LANES:(ci + 1) * LANES]
            a = (a * cos_ref[rs, :] + pltpu.roll(a, LANES - rope_q, 1) * sa_ref[rs, :]
                 + pltpu.roll(a, rope_q, 1) * sb_ref[rs, :])
            o_ref[rs, ci * LANES:(ci + 1) * LANES] = a.astype(BF16)

    def all_chunks(first_col_block):
        for rc in range(tm // rows):
            chunk(rc, first_col_block)

    j = pl.program_id(1)
    pl.when(j == 0)(lambda: all_chunks(True))
    pl.when(j != 0)(lambda: all_chunks(False))


def _qkv_call(xa, modr, gain, w, *, layer, n_ctx_rows, seq, n_rope_cols, n_q_cols, qscale, rope_dim):
    t, d = xa.shape
    n = w.shape[1]
    tn = 512
    hn = tn // 2
    tm = min(1024, seq, n_ctx_rows)
    assert t % tm == 0 and n_ctx_rows % tm == 0 and seq % tm == 0 and n % tn == 0
    assert n_q_cols % hn == 0 and n_rope_cols % hn == 0
    ncb = n_ctx_rows // tm
    bpb = seq // tm
    base = layer * COND_ROWS
    tables = _epilogue_tables(seq, rope_dim, qscale, tm)

    def seg(r):
        return jnp.where(r < ncb, COND_ROWS // 2, (r - ncb) // bpb)

    def tab_spec(half):
        def idx(r, j):
            col0 = j * tn + half * hn
            p = (r - ncb) % bpb
            lat = jnp.where(col0 < n_q_cols, p, jnp.where(col0 < n_rope_cols, bpb + p, 2 * bpb))
            ctx = jnp.where(col0 < n_q_cols, 2 * bpb + 1, 2 * bpb)
            return (jnp.where(r >= ncb, lat, ctx), 0)
        return pl.BlockSpec((tm, LANES), idx)

    return pl.pallas_call(
        functools.partial(_qkv_kernel, rope_q=rope_dim // 4, tn=tn),
        grid=(t // tm, n // tn),
        in_specs=[
            pl.BlockSpec((tm, d), lambda r, j: (r, 0)),
            pl.BlockSpec((1, 1, d), lambda r, j: ((base + seg(r)) * N_MOD + 0, 0, 0)),
            pl.BlockSpec((1, 1, d), lambda r, j: ((base + seg(r)) * N_MOD + 1, 0, 0)),
            pl.BlockSpec((1, d), lambda r, j: (0, 0)),
            pl.BlockSpec((d, tn), lambda r, j: (0, j)),
        ] + [tab_spec(0)] * 3 + [tab_spec(1)] * 3,
        out_specs=pl.BlockSpec((tm, tn), lambda r, j: (r, j)),
        out_shape=jax.ShapeDtypeStruct((t, n), BF16),
        scratch_shapes=[pltpu.VMEM((tm, d), BF16)],
        compiler_params=_cparams(2),
        name=f"qkv_proj_l{layer}",
    )(xa, modr, modr, gain.reshape(1, d), w, *tables, *tables)


def _da_body(q_ref, kc_ref, vc_ref, kx_ref, vx_ref, lq1, lk1, lq2, lk2, g_ref, o_ref, *, lam_init, has_x):
    dh = q_ref.shape[1] // 2
    lam = (jnp.exp(jnp.sum(lq1[...] * lk1[...], axis=-1, keepdims=True))
           - jnp.exp(jnp.sum(lq2[...] * lk2[...], axis=-1, keepdims=True)) + lam_init)

    def probs(lo):
        qc = q_ref[:, lo:lo + dh]
        s_c = _dot_nt(qc, kc_ref[:, lo:lo + dh])
        m = jnp.max(s_c, axis=-1, keepdims=True)
        if has_x:
            s_x = _dot_nt(qc, kx_ref[:, lo:lo + dh])
            m = jnp.maximum(m, jnp.max(s_x, axis=-1, keepdims=True))
        e_c = jnp.exp2(s_c - m)
        den = jnp.sum(e_c, axis=-1, keepdims=True)
        e_x = None
        if has_x:
            e_x = jnp.exp2(s_x - m)
            den = den + jnp.sum(e_x, axis=-1, keepdims=True)
            e_x = e_x.astype(BF16)
        return e_c.astype(BF16), e_x, 1.0 / den

    e1c, e1x, r1 = probs(0)
    e2c, e2x, r2 = probs(dh)
    tq = q_ref.shape[0]
    ov = _dot(jnp.concatenate([e1c, e2c], axis=0), vc_ref[...])
    if has_x:
        ov = ov + _dot(jnp.concatenate([e1x, e2x], axis=0), vx_ref[...])
    o = ov[:tq] * r1 - ov[tq:] * (lam * r2)
    ms = jnp.mean(o * o, axis=-1, keepdims=True)
    o = o * lax.rsqrt(ms + RMS_EPS) * g_ref[...] * (1.0 - lam_init)
    o_ref[...] = o.astype(BF16)


def _da_kernel(*refs, lam_init, nqb, has_ctx_steps):
    body = functools.partial(_da_body, *refs, lam_init=lam_init)
    if has_ctx_steps:
        qb = pl.program_id(2)
        pl.when(qb < nqb)(lambda: body(has_x=True))
        pl.when(qb >= nqb)(lambda: body(has_x=False))
    else:
        body(has_x=True)


def _da_attention(qkv, lam_vecs, subln_g, *, layer, batch, seq, n_ctx, need_ctx):
    t, n3 = qkv.shape
    d = n3 // 3
    hw = d // DA_HEADS
    tc_rows = batch * n_ctx
    tq = min(256, seq)
    assert seq % tq == 0 and n_ctx % tq == 0 and tc_rows % seq == 0 and hw % LANES == 0
    lam_init = 0.8 - 0.6 * math.exp(-0.3 * layer)
    nh = DA_HEADS
    small = [v.reshape(1, -1) for v in lam_vecs] + [subln_g.reshape(1, hw)]
    small_specs = [pl.BlockSpec(v.shape, lambda b, h, qb: (0, 0)) for v in small]
    qoff = tc_rows // tq
    nqb = seq // tq
    ncq = n_ctx // tq if need_ctx else 0
    xoff = tc_rows // seq

    def q_row(b, qb):
        lat = qoff + b * nqb + jnp.minimum(qb, nqb - 1)
        return jnp.where(qb < nqb, lat, b * ncq + (qb - nqb)) if need_ctx else lat

    def o_row(b, qb):
        return q_row(b, qb) if need_ctx else b * nqb + qb

    return pl.pallas_call(
        functools.partial(_da_kernel, lam_init=lam_init, nqb=nqb, has_ctx_steps=need_ctx),
        grid=(batch, nh, nqb + ncq),
        in_specs=[
            pl.BlockSpec((tq, hw), lambda b, h, qb: (q_row(b, qb), h)),
            pl.BlockSpec((n_ctx, hw), lambda b, h, qb: (b, nh + h)),
            pl.BlockSpec((n_ctx, hw), lambda b, h, qb: (b, 2 * nh + h)),
            pl.BlockSpec((seq, hw), lambda b, h, qb: (xoff + b, nh + h)),
            pl.BlockSpec((seq, hw), lambda b, h, qb: (xoff + b, 2 * nh + h)),
        ] + small_specs,
        out_specs=pl.BlockSpec((tq, hw), lambda b, h, qb: (o_row(b, qb), h)),
        out_shape=jax.ShapeDtypeStruct((t if need_ctx else batch * seq, d), BF16),
        compiler_params=_cparams(3),
        name=f"diff_attn_l{layer}",
    )(qkv, qkv, qkv, qkv, qkv, *small)


def _gw_body(sink_ref, q_ref, kc_ref, vc_ref, kx_ref, vx_ref, o_ref, *, has_x, heads_per_step, group, dim, nqb):
    gp = pl.program_id(1)
    tq = q_ref.shape[0]
    half = LANES // 2
    assert dim == half
    tiles = group * dim // LANES
    kv_heads = heads_per_step // group
    if has_x:
        qb = pl.program_id(2)
        st_p = pl.multiple_of(jnp.maximum(qb - 1, 0) * tq, tq)
        st_m = pl.multiple_of(qb * tq, tq)
        st_n = pl.multiple_of(jnp.minimum(qb + 1, nqb - 1) * tq, tq)
        k_all = jnp.concatenate([kc_ref[...]] + [kx_ref[pl.ds(s, tq), :] for s in (st_p, st_m, st_n)], axis=0)
        v_all = jnp.concatenate([vc_ref[...]] + [vx_ref[pl.ds(s, tq), :] for s in (st_p, st_m, st_n)], axis=0)
        n_ctx = kc_ref.shape[0]
        r = lax.broadcasted_iota(jnp.int32, (tq, tq), 0)
        c = lax.broadcasted_iota(jnp.int32, (tq, tq), 1)
        zero = jnp.zeros((tq, tq), F32)
        bias_p = jnp.where(jnp.logical_and(c >= r, qb >= 1), 0.0, NEG_INF)
        bias_n = jnp.where(jnp.logical_and(c <= r, qb + 1 < nqb), 0.0, NEG_INF)
        bias = jnp.concatenate([jnp.zeros((tq, n_ctx), F32), bias_p, zero, bias_n], axis=1)
        bias = jnp.concatenate([bias] * tiles, axis=0)
    else:
        k_all = kc_ref[...]
        v_all = vc_ref[...]
        bias = None
    lo = lax.broadcasted_iota(jnp.int32, k_all.shape, 1) < half

    def half_variants(x):
        xf = x.astype(F32)
        xs = pltpu.roll(xf, half, 1)
        keep_lo = lambda a: jnp.where(lo, a, 0.0).astype(BF16)
        keep_hi = lambda a: jnp.where(lo, 0.0, a).astype(BF16)
        return [(keep_lo(xf), keep_hi(xs)), (keep_lo(xs), keep_hi(xf))]

    k_var = half_variants(k_all)
    v_var = half_variants(v_all)
    for gi in range(kv_heads):
        kz = k_var[gi]
        vz = v_var[gi]
        q_stack = jnp.concatenate(
            [q_ref[:, (gi * tiles + j) * LANES:(gi * tiles + j + 1) * LANES] for j in range(tiles)], axis=0)
        out = None
        for par in range(2):
            s = _dot_nt(q_stack, kz[par])
            if bias is not None:
                s = s + bias
            head0 = gp * heads_per_step + gi * group + par
            sk = jnp.concatenate(
                [jnp.full((tq, 1), sink_ref[head0 + 2 * j] * LOG2E, F32) for j in range(tiles)], axis=0)
            m = jnp.maximum(jnp.max(s, axis=-1, keepdims=True), sk)
            e = jnp.exp2(s - m)
            den = jnp.sum(e, axis=-1, keepdims=True) + jnp.exp2(sk - m)
            o = _dot(e.astype(BF16), vz[par]) * (1.0 / den)
            out = o if out is None else out + o
        for j in range(tiles):
            o_ref[:, (gi * tiles + j) * LANES:(gi * tiles + j + 1) * LANES] = out[j * tq:(j + 1) * tq].astype(BF16)


def _gw_kernel(*refs, nqb, has_ctx_steps, **kw):
    body = functools.partial(_gw_body, *refs, nqb=nqb, **kw)
    if has_ctx_steps:
        qb = pl.program_id(2)
        pl.when(qb < nqb)(lambda: body(has_x=True))
        pl.when(qb >= nqb)(lambda: body(has_x=False))
    else:
        body(has_x=True)


def _gw_attention(qkv, sinks, *, layer, batch, seq, n_ctx, need_ctx, d):
    t = qkv.shape[0]
    dim = d // GW_HEADS
    kv_per_step = LANES // dim
    hps = kv_per_step * GW_GROUP
    qw = hps * dim
    n_gp = GW_KV_HEADS // kv_per_step
    tq = WINDOW
    tc_rows = batch * n_ctx
    assert seq % tq == 0 and n_ctx % tq == 0 and tc_rows % seq == 0
    nqb = seq // tq
    ncq = n_ctx // tq if need_ctx else 0
    qoff = tc_rows // tq
    xoff = tc_rows // seq
    kcol = d // LANES
    vcol = kcol + GW_KV_HEADS * dim // LANES

    def q_row(b, qb):
        lat = qoff + b * nqb + jnp.minimum(qb, nqb - 1)
        return jnp.where(qb < nqb, lat, b * ncq + (qb - nqb)) if need_ctx else lat

    def o_row(b, qb):
        return q_row(b, qb) if need_ctx else b * nqb + qb

    return pl.pallas_call(
        functools.partial(_gw_kernel, nqb=nqb, has_ctx_steps=need_ctx, heads_per_step=hps, group=GW_GROUP, dim=dim),
        grid=(batch, n_gp, nqb + ncq),
        in_specs=[
            pl.BlockSpec(memory_space=pltpu.SMEM),
            pl.BlockSpec((tq, qw), lambda b, g, qb: (q_row(b, qb), g)),
            pl.BlockSpec((n_ctx, LANES), lambda b, g, qb: (b, kcol + g)),
            pl.BlockSpec((n_ctx, LANES), lambda b, g, qb: (b, vcol + g)),
            pl.BlockSpec((seq, LANES), lambda b, g, qb: (xoff + b, kcol + g)),
            pl.BlockSpec((seq, LANES), lambda b, g, qb: (xoff + b, vcol + g)),
        ],
        out_specs=pl.BlockSpec((tq, qw), lambda b, g, qb: (o_row(b, qb), g)),
        out_shape=jax.ShapeDtypeStruct((t if need_ctx else batch * seq, d), BF16),
        compiler_params=_cparams(3),
        name=f"gqa_attn_l{layer}",
    )(sinks, qkv, qkv, qkv, qkv, qkv)


def _pack_bf16_pairs(h):
    half = h.shape[1] // 2
    hi = lax.bitcast_convert_type(h[:, :half].astype(BF16).astype(F32), jnp.uint32)
    lo = lax.bitcast_convert_type(h[:, half:].astype(BF16).astype(F32), jnp.uint32)
    return hi | (lo >> 16)


def _unpack_bf16_pairs(u, dtype=BF16):
    hi = lax.bitcast_convert_type(u & jnp.uint32(0xFFFF0000), F32)
    lo = lax.bitcast_convert_type(u << 16, F32)
    return jnp.concatenate([hi, lo], axis=1).astype(dtype)


def _oproj_kernel(o_ref, w_ref, x_ref, g1_ref, sh_ref, sc_ref, gn_ref, rcat_ref, rhi_ref,
                  xo_ref, h_ref, lg_ref):
    tm = o_ref.shape[0]
    rows = min(tm, EPILOGUE_ROWS)
    for rc in range(tm // rows):
        rs = slice(rc * rows, (rc + 1) * rows)
        xn = x_ref[rs, :] + g1_ref[0] * _dot(o_ref[rs, :], w_ref[...])
        xo_ref[rs, :] = xn
        ms = jnp.mean(xn * xn, axis=-1, keepdims=True)
        h = xn * lax.rsqrt(ms + RMS_EPS) * gn_ref[...]
        h = h * (1.0 + sc_ref[0]) + sh_ref[0]
        h_ref[rs, :] = _pack_bf16_pairs(h)
        h_hi = h.astype(BF16)
        h_lo = (h - h_hi.astype(F32)).astype(BF16)
        hw = _dot(h_hi, rcat_ref[...])
        lg = hw[:, :ROUTER_LANES] + (hw[:, ROUTER_LANES:] + _dot(h_lo, rhi_ref[...]))
        lg_ref[rs, :] = _route_rows(lg)


def _route_rows(lg):
    lane = lax.broadcasted_iota(jnp.int32, lg.shape, 1).astype(F32)
    far = jnp.float32(2 * ROUTER_LANES)

    def first_lane(cond):
        return jnp.min(jnp.where(cond, lane, far), axis=-1, keepdims=True)

    is_g = lane < N_GROUPS
    gl = jnp.where(is_g, lg, NEG_INF)
    gmax = jnp.max(gl, axis=-1, keepdims=True)
    g_sel = first_lane(jnp.logical_and(is_g, gl == gmax))
    g_w = 1.0 / jnp.sum(jnp.where(is_g, jnp.exp(gl - gmax), 0.0), axis=-1, keepdims=True)
    lo_e = N_GROUPS + g_sel * EXPERTS_PER_GROUP
    in_grp = jnp.logical_and(lane >= lo_e, lane < lo_e + EXPERTS_PER_GROUP)
    el = jnp.where(in_grp, lg, NEG_INF)
    ee = jnp.where(in_grp, jnp.exp(el - jnp.max(el, axis=-1, keepdims=True)), 0.0)
    prob = ee / jnp.sum(ee, axis=-1, keepdims=True)
    p1 = jnp.max(jnp.where(in_grp, prob, -1.0), axis=-1, keepdims=True)
    i1 = first_lane(jnp.logical_and(in_grp, prob == p1))
    rest = jnp.where(jnp.logical_and(in_grp, lane != i1), prob, -1.0)
    p2 = jnp.max(rest, axis=-1, keepdims=True)
    i2 = first_lane(jnp.logical_and(rest >= 0.0, rest == p2))
    den = p1 + p2
    vals = (g_w * (p1 / den), g_w * (p2 / den), i1 - N_GROUPS, i2 - N_GROUPS)
    out = jnp.zeros_like(lg)
    for k, v in enumerate(vals):
        out = jnp.where(lane == k, v, out)
    return out


def _oproj_call(o, w, xa, modr, gain, r_hi, r_lo, *, layer, n_ctx_rows, seq, latent_only):
    t, d = xa.shape
    tm = 512
    assert n_ctx_rows % tm == 0 and seq % tm == 0
    ncb = n_ctx_rows // tm
    bpb = seq // tm
    off = ncb if latent_only else 0
    nblk = t // tm - off
    base = layer * COND_ROWS

    def mrow(r, chunk):
        blk = r + off
        seg = jnp.where(blk < ncb, COND_ROWS // 2, (blk - ncb) // bpb)
        return ((base + seg) * N_MOD + chunk, 0, 0)

    const = lambda r: (0, 0)
    return pl.pallas_call(
        _oproj_kernel,
        grid=(nblk,),
        in_specs=[
            pl.BlockSpec((tm, d), lambda r: (r, 0)),
            pl.BlockSpec((d, d), const, pipeline_mode=pl.Buffered(1)),
            pl.BlockSpec((tm, d), lambda r: (r + off, 0)),
            pl.BlockSpec((1, 1, d), lambda r: mrow(r, 2)),
            pl.BlockSpec((1, 1, d), lambda r: mrow(r, 3)),
            pl.BlockSpec((1, 1, d), lambda r: mrow(r, 4)),
            pl.BlockSpec((1, d), const),
            pl.BlockSpec((d, 2 * ROUTER_LANES), const),
            pl.BlockSpec((d, ROUTER_LANES), const),
        ],
        out_specs=[
            pl.BlockSpec((tm, d), lambda r: (r + off, 0)),
            pl.BlockSpec((tm, d // 2), lambda r: (r, 0)),
            pl.BlockSpec((tm, ROUTER_LANES), lambda r: (r, 0)),
        ],
        out_shape=[
            jax.ShapeDtypeStruct((t, d), F32),
            jax.ShapeDtypeStruct((nblk * tm, d // 2), jnp.uint32),
            jax.ShapeDtypeStruct((nblk * tm, ROUTER_LANES), F32),
        ],
        input_output_aliases={2: 0},
        compiler_params=_cparams(1),
        name=f"out_proj_l{layer}",
    )(o, w, xa, modr, modr, modr, gain.reshape(1, d), jnp.concatenate([r_hi, r_lo], axis=1), r_hi)


def _inclusive_cumsum_rows(m):
    n, k = m.shape
    blk = LANES
    assert n % blk == 0
    tri = (jnp.arange(blk)[:, None] >= jnp.arange(blk)[None, :]).astype(F32)
    m3 = m.reshape(n // blk, blk, k)
    within = jnp.einsum("ij,bjk->bik", tri, m3, precision=lax.Precision.HIGHEST)
    sums = within[:, -1, :]
    nb = n // blk
    tri_b = (jnp.arange(nb)[:, None] > jnp.arange(nb)[None, :]).astype(F32)
    offs = jnp.dot(tri_b, sums, precision=lax.Precision.HIGHEST)
    return (within + offs[:, None, :]).reshape(n, k)


def _route(routed, tb):
    t = routed.shape[0]
    w = routed[:, :TOP_K]
    eid = routed[:, TOP_K:2 * TOP_K].astype(jnp.int32).reshape(-1)
    a = t * TOP_K
    onehot = (eid[:, None] == jnp.arange(N_EXPERTS, dtype=jnp.int32)[None]).astype(F32)
    csum = _inclusive_cumsum_rows(onehot)
    counts = csum[-1].astype(jnp.int32)
    nblk = (counts + tb - 1) // tb
    bend = jnp.cumsum(nblk)
    bstart = bend - nblk
    slot = jnp.sum(onehot * (csum - 1.0 + (bstart * tb).astype(F32)[None]), axis=-1)
    dest = slot.astype(jnp.int32)
    n_blocks = -(-a // tb) + N_EXPERTS
    blocks = jnp.arange(n_blocks, dtype=jnp.int32)
    block_expert = jnp.minimum(jnp.sum((bend[None, :] <= blocks[:, None]).astype(jnp.int32), axis=1), N_EXPERTS - 1)
    nact = bend[-1:].astype(jnp.int32)
    sorted_tok = (jnp.argsort(eid, stable=True) // TOP_K).astype(jnp.int32)
    start = jnp.cumsum(counts) - counts
    slot_e = jnp.repeat(block_expert, tb)
    off = jnp.arange(n_blocks * tb, dtype=jnp.int32) - bstart[slot_e] * tb
    src = jnp.clip(start[slot_e] + off, 0, a - 1)
    slot_tok = jnp.where(off < counts[slot_e], sorted_tok[src], 0)
    return w, dest, slot_tok, block_expert, nact, n_blocks


GATHER_LAG = 2


def _moe_kernel(be_ref, nact_ref, st_ref, h_hbm, wg_ref, wu_ref, wd_ref, y_ref, xbuf, wg_s, wu_s, wd_s, sem):
    s = pl.program_id(0)
    nact = nact_ref[0]
    nbuf, tb = xbuf.shape[0], xbuf.shape[1]
    blk = s - GATHER_LAG
    do_gather = s < nact
    do_ffn = jnp.logical_and(blk >= 0, blk < nact)
    prev_blk = jnp.maximum(blk - 1, 0)
    fresh = jnp.logical_or(blk == 0, be_ref[jnp.maximum(blk, 0)] != be_ref[prev_blk])

    def gather_row(r, slot):
        pltpu.make_async_copy(h_hbm.at[pl.ds(st_ref[0, 0, r], 1)], xbuf.at[slot, pl.ds(r, 1)], sem.at[slot]).start()

    def gather_rows():
        for r in range(tb):
            gather_row(r, s % nbuf)

    def gather_rows_rolled():
        lax.fori_loop(0, tb, lambda r, c: (gather_row(r, s % nbuf), c)[1], 0, unroll=8)

    def ffn(also_gather):
        slot = blk % nbuf
        pltpu.make_async_copy(h_hbm.at[pl.ds(0, tb)], xbuf.at[slot], sem.at[slot]).wait()
        xb = _unpack_bf16_pairs(xbuf[slot])
        if also_gather:
            gather_rows()
        g = _dot(xb, wg_s[...])
        u = _dot(xb, wu_s[...])
        act = (g / (1.0 + jnp.exp(-g))) * u
        y_ref[...] = _pack_bf16_pairs(_dot(act.astype(BF16), wd_s[...]))

    @pl.when(jnp.logical_and(do_ffn, fresh))
    def _():
        wg_s[...] = wg_ref[...].astype(BF16)
        wu_s[...] = wu_ref[...].astype(BF16)
        wd_s[...] = wd_ref[...].astype(BF16)

    pl.when(jnp.logical_and(do_ffn, do_gather))(lambda: ffn(True))
    pl.when(jnp.logical_and(do_ffn, jnp.logical_not(do_gather)))(lambda: ffn(False))
    pl.when(jnp.logical_and(jnp.logical_not(do_ffn), do_gather))(gather_rows_rolled)

    @pl.when(jnp.logical_and(blk >= 0, blk >= nact))
    def _():
        y_ref[...] = jnp.zeros_like(y_ref)


def _moe_call(hp, slot_tok, block_expert, nact, wg, wu, wd, *, n_blocks, tb, layer):
    dw = hp.shape[1]
    d, de = wg.shape[2], wg.shape[3]
    prev = lambda s: jnp.maximum(s - GATHER_LAG, 0)
    grid_spec = pltpu.PrefetchScalarGridSpec(
        num_scalar_prefetch=2,
        grid=(n_blocks + GATHER_LAG,),
        in_specs=[
            pl.BlockSpec((1, 1, tb), lambda s, be, na: (jnp.minimum(s, n_blocks - 1), 0, 0),
                         memory_space=pltpu.SMEM),
            pl.BlockSpec(memory_space=pl.ANY),
            pl.BlockSpec((None, None, d, de), lambda s, be, na: (layer, be[prev(s)], 0, 0)),
            pl.BlockSpec((None, None, d, de), lambda s, be, na: (layer, be[prev(s)], 0, 0)),
            pl.BlockSpec((None, None, de, d), lambda s, be, na: (layer, be[prev(s)], 0, 0)),
        ],
        out_specs=pl.BlockSpec((tb, d // 2), lambda s, be, na: (prev(s), 0)),
        scratch_shapes=[pltpu.VMEM((GATHER_LAG + 1, tb, dw), jnp.uint32), pltpu.VMEM((d, de), BF16),
                        pltpu.VMEM((d, de), BF16), pltpu.VMEM((de, d), BF16),
                        pltpu.SemaphoreType.DMA((GATHER_LAG + 1,))],
    )
    return pl.pallas_call(
        _moe_kernel,
        grid_spec=grid_spec,
        out_shape=jax.ShapeDtypeStruct((n_blocks * tb, d // 2), jnp.uint32),
        compiler_params=_cparams(1),
        name=f"moe_experts_l{layer}",
    )(block_expert, nact, slot_tok.reshape(n_blocks, 1, tb), hp, wg, wu, wd)


def _combine_kernel(pos_ref, y_hbm, w_ref, x_ref, g2_ref, gout_ref, o_ref, ybuf, sem, *, final_norm):
    s = pl.program_id(0)
    nb = pl.num_programs(0) - 1
    tc = ybuf.shape[2]

    def gather_row(r, k):
        slot = s % 2
        pltpu.make_async_copy(y_hbm.at[pl.ds(pos_ref[0, 0, r * TOP_K + k], 1)], ybuf.at[slot, k, pl.ds(r, 1)],
                              sem.at[slot]).start()

    def gather_rows():
        for r in range(tc):
            for k in range(TOP_K):
                gather_row(r, k)

    def gather_rows_rolled():
        def body(r, c):
            for k in range(TOP_K):
                gather_row(r, k)
            return c
        lax.fori_loop(0, tc, body, 0, unroll=8)

    def finish(also_gather):
        slot = (s - 1) % 2
        for k in range(TOP_K):
            pltpu.make_async_copy(y_hbm.at[pl.ds(0, tc)], ybuf.at[slot, k], sem.at[slot]).wait()
        if also_gather:
            gather_rows()
        w = w_ref[...]
        moe = (_unpack_bf16_pairs(ybuf[slot, 0], F32) * w[:, 0:1]
               + _unpack_bf16_pairs(ybuf[slot, 1], F32) * w[:, 1:2])
        xn = x_ref[...] + g2_ref[0] * moe
        if final_norm:
            ms = jnp.mean(xn * xn, axis=-1, keepdims=True)
            xn = xn * lax.rsqrt(ms + RMS_EPS) * gout_ref[...]
        o_ref[...] = xn

    pl.when(s == 0)(gather_rows_rolled)
    pl.when(jnp.logical_and(s >= 1, s < nb))(lambda: finish(True))
    pl.when(s == nb)(lambda: finish(False))


def _combine_call(y, dest, w, xa, modr, gout, *, layer, n_ctx_rows, seq, latent_only, final_norm):
    t, d = xa.shape
    tc = 256
    assert n_ctx_rows % tc == 0 and seq % tc == 0
    ncb = n_ctx_rows // tc
    bpb = seq // tc
    off = ncb if latent_only else 0
    nblk = t // tc - off
    base = layer * COND_ROWS
    prev = lambda s: jnp.maximum(s - 1, 0)

    def mrow(s):
        blk = prev(s) + off
        seg = jnp.where(blk < ncb, COND_ROWS // 2, (blk - ncb) // bpb)
        return ((base + seg) * N_MOD + 5, 0, 0)

    out_off = 0 if final_norm else off
    out_rows = nblk * tc if final_norm else t
    return pl.pallas_call(
        functools.partial(_combine_kernel, final_norm=final_norm),
        grid=(nblk + 1,),
        in_specs=[
            pl.BlockSpec((1, 1, TOP_K * tc), lambda s: (jnp.minimum(s, nblk - 1), 0, 0), memory_space=pltpu.SMEM),
            pl.BlockSpec(memory_space=pl.ANY),
            pl.BlockSpec((tc, TOP_K), lambda s: (prev(s), 0)),
            pl.BlockSpec((tc, d), lambda s: (prev(s) + off, 0)),
            pl.BlockSpec((1, 1, d), mrow),
            pl.BlockSpec((1, d), lambda s: (0, 0)),
        ],
        out_specs=pl.BlockSpec((tc, d), lambda s: (prev(s) + out_off, 0)),
        out_shape=jax.ShapeDtypeStruct((out_rows, d), F32),
        scratch_shapes=[pltpu.VMEM((2, TOP_K, tc, y.shape[1]), y.dtype), pltpu.SemaphoreType.DMA((2,))],
        input_output_aliases={} if final_norm else {3: 0},
        compiler_params=_cparams(1),
        name=f"moe_combine_l{layer}",
    )(dest.reshape(nblk, 1, TOP_K * tc), y, w, xa, modr, gout.reshape(1, d))


def kernel(x, c, ctx, c_ctx, ada_w, ada_b, norm_mix_g, norm_ffn_g, norm_out_g, da_w_in, da_w_out,
           da_lam_q1, da_lam_k1, da_lam_q2, da_lam_k2, da_subln_g, gw_w_in, gw_w_out, gw_sinks,
           moe_w_group, moe_w_expert, moe_w_gate, moe_w_up, moe_w_down):
    batch, seq, d = x.shape
    n_ctx = ctx.shape[1]
    tc_rows = batch * n_ctx
    depth = ada_w.shape[0]
    assert batch < COND_ROWS // 2 + 1 and depth == DEPTH

    xa = jnp.concatenate([ctx.reshape(tc_rows, d), x.reshape(batch * seq, d)], axis=0)
    cond = jnp.zeros((COND_ROWS, d), F32).at[:batch].set(c).at[COND_ROWS // 2].set(c_ctx)
    mod = _ada_call(cond, ada_w, ada_b)
    modr = mod.reshape(depth * COND_ROWS * N_MOD, 1, d)

    da_dim = d // (2 * DA_HEADS)
    gw_dim = d // GW_HEADS
    tb = 256
    out = None
    for i in range(depth):
        last = i == depth - 1
        need_ctx = not last
        jm = i // N_MIXERS
        if i % N_MIXERS == 0:
            qkv = _qkv_call(xa, modr, norm_mix_g[i], da_w_in[jm].astype(BF16), layer=i,
                            n_ctx_rows=tc_rows, seq=seq, n_rope_cols=2 * d, n_q_cols=d,
                            qscale=da_dim ** -0.5 * LOG2E, rope_dim=da_dim)
            o = _da_attention(qkv, (da_lam_q1[jm], da_lam_k1[jm], da_lam_q2[jm], da_lam_k2[jm]),
                              da_subln_g[jm], layer=i, batch=batch, seq=seq, n_ctx=n_ctx, need_ctx=need_ctx)
            w_out = da_w_out[jm]
        else:
            kv_w = GW_KV_HEADS * gw_dim
            qkv = _qkv_call(xa, modr, norm_mix_g[i], gw_w_in[jm].astype(BF16), layer=i,
                            n_ctx_rows=tc_rows, seq=seq, n_rope_cols=d + kv_w, n_q_cols=d,
                            qscale=gw_dim ** -0.5 * LOG2E, rope_dim=gw_dim)
            o = _gw_attention(qkv, gw_sinks[jm], layer=i, batch=batch, seq=seq, n_ctx=n_ctx,
                              need_ctx=need_ctx, d=d)
            w_out = gw_w_out[jm]
        w_r = jnp.concatenate([moe_w_group[i], moe_w_expert[i]], axis=1)
        w_r = jnp.pad(w_r, ((0, 0), (0, ROUTER_LANES - w_r.shape[1])))
        r_hi = w_r.astype(BF16)
        r_lo = (w_r - r_hi.astype(F32)).astype(BF16)
        xa, h2, logits = _oproj_call(o, w_out.astype(BF16), xa, modr, norm_ffn_g[i], r_hi, r_lo, layer=i,
                                     n_ctx_rows=tc_rows, seq=seq, latent_only=last)
        w_tok, dest, slot_tok, block_expert, nact, n_blocks = _route(logits, tb)
        y = _moe_call(h2, slot_tok, block_expert, nact, moe_w_gate, moe_w_up, moe_w_down,
                      n_blocks=n_blocks, tb=tb, layer=i)
        res = _combine_call(y, dest, w_tok, xa, modr, norm_out_g, layer=i, n_ctx_rows=tc_rows, seq=seq,
                            latent_only=last, final_norm=last)
        if last:
            out = res
        else:
            xa = res
    return out.reshape(batch, seq, d)
```
